```python
import jax
import jax.numpy as jnp
from jax import lax
import numpy as np

D_MODEL = 1024
BATCH = 8
SEQ = 8192
DEPTH = 1
DEC_BATCH = 16
DEC_SEQ = 64
PAST_LEN = 1024

CHUNK = 64
EPS = 1e-6
A_HEADS = 8
A_HEAD_DIM = 64
A_WIDTH = A_HEADS * A_HEAD_DIM
BAND_CHUNKS = 8
BAND_PAST = BAND_CHUNKS * CHUNK
BAND_LEN = BAND_PAST + CHUNK
REL_CLIP = 128
MASK_VALUE = -1e30
R_HEADS = 4
R_KEY_DIM = 128
R_VAL_DIM = 128
R_QK_WIDTH = R_HEADS * R_KEY_DIM
R_V_WIDTH = R_HEADS * R_VAL_DIM
ROPE_BASE = 10000.0
P_HEADS = 8
P_KEYS = 128
P_EXPERTS = P_KEYS * P_KEYS
P_QUERY_DIM = 256
P_HALF = P_QUERY_DIM // 2
P_TOPK = 16
P_TOKEN_BLOCK = 128
IN_SIZES = (A_WIDTH, A_WIDTH, A_WIDTH, R_QK_WIDTH, R_QK_WIDTH, R_V_WIDTH, R_V_WIDTH, D_MODEL, D_MODEL)
IN_WIDTH = sum(IN_SIZES)

kernel_name = "streaming_band_attn_retention_peer"


def rms_norm(x, g):
    xf = x.astype(jnp.float32)
    y = xf * lax.rsqrt(jnp.mean(xf * xf, axis=-1, keepdims=True) + EPS)
    return (y * g.astype(jnp.float32)).astype(x.dtype)


def rotary(x, pos):
    half = x.shape[-1] // 2
    inv = ROPE_BASE ** (-jnp.arange(half, dtype=jnp.float32) / half)
    ang = pos.astype(jnp.float32)[:, None] * inv[None, :]
    cos = jnp.cos(ang)[None, :, None, :]
    sin = jnp.sin(ang)[None, :, None, :]
    xf = x.astype(jnp.float32)
    x1, x2 = xf[..., :half], xf[..., half:]
    return jnp.concatenate([x1 * cos - x2 * sin, x1 * sin + x2 * cos], axis=-1).astype(x.dtype)


def mixer_inputs(h, pos, w_in, q_norm_g, k_norm_g):
    B, T, _ = h.shape
    z = h @ w_in
    offs = [int(o) for o in np.cumsum(IN_SIZES)[:-1]]
    qa, ka, va, qr, kr, vr, gr, gate_a, gate_r = jnp.split(z, offs, axis=-1)
    qa = rms_norm(qa.reshape(B, T, A_HEADS, A_HEAD_DIM), q_norm_g)
    ka = rms_norm(ka.reshape(B, T, A_HEADS, A_HEAD_DIM), k_norm_g)
    va = va.reshape(B, T, A_HEADS, A_HEAD_DIM)
    qr = rotary(qr.reshape(B, T, R_HEADS, R_KEY_DIM), pos)
    kr = rotary(kr.reshape(B, T, R_HEADS, R_KEY_DIM), pos) * (R_KEY_DIM ** -0.5)
    vr = vr.reshape(B, T, R_HEADS, R_VAL_DIM)
    return qa, ka, va, qr, kr, vr, gr, gate_a, gate_r


def band_attention(q, k, v, qpos, kpos, kvalid, rel_bias):
    s = jnp.einsum('bqhd,bkhd->bhqk', q, k).astype(jnp.float32) * (A_HEAD_DIM ** -0.5)
    rel = jnp.clip(qpos[:, None] - kpos[None, :], -REL_CLIP, REL_CLIP) + REL_CLIP
    s = s + rel_bias.astype(jnp.float32)[:, rel]
    s = jnp.where(kvalid[None, None, None, :], s, MASK_VALUE)
    p = jax.nn.softmax(s, axis=-1).astype(v.dtype)
    return jnp.einsum('bhqk,bkhd->bqhd', p, v)


def attention_prompt(q, k, v, rel_bias):
    B, S, H, Dh = q.shape
    nc = S // CHUNK
    kp = jnp.pad(k, ((0, 0), (BAND_PAST, 0), (0, 0), (0, 0)))
    vp = jnp.pad(v, ((0, 0), (BAND_PAST, 0), (0, 0), (0, 0)))
    qc = q.reshape(B, nc, CHUNK, H, Dh).swapaxes(0, 1)
    local_q = jnp.arange(CHUNK)
    local_k = jnp.arange(BAND_LEN) - BAND_PAST

    def one_chunk(args):
        c, qb = args
        start = c * CHUNK
        kb = lax.dynamic_slice_in_dim(kp, start, BAND_LEN, axis=1)
        vb = lax.dynamic_slice_in_dim(vp, start, BAND_LEN, axis=1)
        kvalid = (start + local_k) >= 0
        return band_attention(qb, kb, vb, local_q, local_k, kvalid, rel_bias)

    out = lax.map(one_chunk, (jnp.arange(nc), qc))
    return out.swapaxes(0, 1).reshape(B, S, H, Dh)


def retention_block(q, k, v, state):
    L = q.shape[1]
    dt = q.dtype
    log_gamma = jnp.log1p(-(2.0 ** (-5.0 - jnp.arange(R_HEADS, dtype=jnp.float32))))
    idx = jnp.arange(L, dtype=jnp.float32)
    diff = idx[:, None] - idx[None, :]
    decay = jnp.where(diff >= 0, jnp.exp(log_gamma[:, None, None] * jnp.maximum(diff, 0.0)), 0.0)
    inner = jnp.einsum('blhd,bmhd->bhlm', q, k) * decay.astype(dt)[None]
    o = jnp.einsum('bhlm,bmhe->blhe', inner, v)
    q_decay = jnp.exp(log_gamma[None, :] * (idx[:, None] + 1.0)).astype(dt)
    o = o + jnp.einsum('blhd,bhde->blhe', q, state) * q_decay[None, :, :, None]
    k_decay = jnp.exp(log_gamma[None, :] * (L - 1.0 - idx)[:, None]).astype(dt)
    state_decay = jnp.exp(log_gamma * L).astype(dt)
    new_state = state * state_decay[None, :, None, None] + jnp.einsum('blhd,lh,blhe->bhde', k, k_decay, v)
    return o, new_state


def retention_prompt(q, k, v):
    B, S, H, dk = q.shape
    nc = S // CHUNK

    def to_chunks(a):
        return a.reshape(B, nc, CHUNK, H, a.shape[-1]).swapaxes(0, 1)

    s0 = jnp.zeros((B, H, dk, v.shape[-1]), q.dtype)

    def step(s, qkv):
        o, s_new = retention_block(qkv[0], qkv[1], qkv[2], s)
        return s_new, o

    s_fin, o = lax.scan(step, s0, (to_chunks(q), to_chunks(k), to_chunks(v)))
    return o.swapaxes(0, 1).reshape(B, S, H, v.shape[-1]), s_fin


def peer(h, w_query, sub_keys, expert_u, expert_v):
    n = h.shape[0]
    nb = -(-n // P_TOKEN_BLOCK)
    hp = jnp.pad(h, ((0, nb * P_TOKEN_BLOCK - n), (0, 0))).reshape(nb, P_TOKEN_BLOCK, D_MODEL)

    def one_block(hb):
        t = hb.shape[0]
        qry = (hb @ w_query).reshape(t, P_HEADS, 2, P_HALF)
        s = jnp.einsum('thcd,hckd->thck', qry, sub_keys).astype(jnp.float32)
        top_s, top_i = lax.top_k(s, P_TOPK)
        cand_s = (top_s[:, :, 0, :, None] + top_s[:, :, 1, None, :]).reshape(t, P_HEADS, P_TOPK * P_TOPK)
        cand_i = (top_i[:, :, 0, :, None] * P_KEYS + top_i[:, :, 1, None, :]).reshape(t, P_HEADS, P_TOPK * P_TOPK)
        best_s, best_pos = lax.top_k(cand_s, P_TOPK)
        expert = jnp.take_along_axis(cand_i, best_pos, axis=-1)
        gate = jax.nn.softmax(best_s, axis=-1)
        a = jnp.einsum('td,thkd->thk', hb, expert_u[expert]).astype(jnp.float32)
        act = (jax.nn.gelu(a, approximate=False) * gate).astype(hb.dtype)
        return jnp.einsum('thk,thkd->td', act, expert_v[expert])

    out = lax.map(one_block, hp)
    return out.reshape(nb * P_TOKEN_BLOCK, D_MODEL)[:n]


def layer_forward(x, pos, attend, retain, p):
    (mix_g, w_in, q_g, k_g, rel_b, r_g, w_ba, w_br, w_o, ffn_g, w_q, s_keys, e_u, e_v) = p
    B, T, _ = x.shape
    h = rms_norm(x, mix_g)
    qa, ka, va, qr, kr, vr, gr, gate_a, gate_r = mixer_inputs(h, pos, w_in, q_g, k_g)
    oa, k_new, v_new = attend(qa, ka, va, rel_b)
    orr, s_new = retain(qr, kr, vr)
    branch_a = oa.reshape(B, T, A_WIDTH) @ w_ba
    branch_r = (rms_norm(orr, r_g).reshape(B, T, R_V_WIDTH) * jax.nn.silu(gr)) @ w_br
    merged = jax.nn.sigmoid(gate_a) * branch_a + jax.nn.sigmoid(gate_r) * branch_r
    x = x + merged @ w_o
    h2 = rms_norm(x, ffn_g).reshape(B * T, D_MODEL)
    x = x + peer(h2, w_q, s_keys, e_u, e_v).reshape(B, T, D_MODEL)
    return x, k_new, v_new, s_new


def setup_inputs(seed: int = 0) -> dict:
    key = jax.random.key(seed)
    ks = jax.random.split(key, 20)
    f32 = jnp.float32

    def nrm(k, shape, scale):
        return jax.random.normal(k, shape, f32) * scale

    rows = min(BAND_PAST, PAST_LEN)
    return {
        "x_prompt": nrm(ks[0], (BATCH, SEQ, D_MODEL), 1.0),
        "x_sample": nrm(ks[1], (DEC_BATCH, DEC_SEQ, D_MODEL), 1.0),
        "cache_attn_k": nrm(ks[2], (DEPTH, DEC_BATCH, rows, A_HEADS, A_HEAD_DIM), 1.0),
        "cache_attn_v": nrm(ks[3], (DEPTH, DEC_BATCH, rows, A_HEADS, A_HEAD_DIM), 1.0),
        "state_retention": nrm(ks[4], (DEPTH, DEC_BATCH, R_HEADS, R_KEY_DIM, R_VAL_DIM), 0.5),
        "mix_norm_g": 1.0 + nrm(ks[5], (DEPTH, D_MODEL), 0.02),
        "w_in": nrm(ks[6], (DEPTH, D_MODEL, IN_WIDTH), D_MODEL ** -0.5),
        "attn_q_norm_g": 1.0 + nrm(ks[7], (DEPTH, A_HEAD_DIM), 0.02),
        "attn_k_norm_g": 1.0 + nrm(ks[8], (DEPTH, A_HEAD_DIM), 0.02),
        "attn_rel_bias": nrm(ks[9], (DEPTH, A_HEADS, 2 * REL_CLIP + 1), 0.1),
        "ret_out_norm_g": 1.0 + nrm(ks[10], (DEPTH, R_HEADS, R_VAL_DIM), 0.02),
        "w_branch_attn": nrm(ks[11], (DEPTH, A_WIDTH, D_MODEL), A_WIDTH ** -0.5),
        "w_branch_ret": nrm(ks[12], (DEPTH, R_V_WIDTH, D_MODEL), R_V_WIDTH ** -0.5),
        "w_out": nrm(ks[13], (DEPTH, D_MODEL, D_MODEL), D_MODEL ** -0.5),
        "ffn_norm_g": 1.0 + nrm(ks[14], (DEPTH, D_MODEL), 0.02),
        "peer_w_query": nrm(ks[15], (DEPTH, D_MODEL, P_HEADS * P_QUERY_DIM), D_MODEL ** -0.5),
        "peer_sub_keys": nrm(ks[16], (DEPTH, P_HEADS, 2, P_KEYS, P_HALF), P_HALF ** -0.5),
        "peer_expert_u": nrm(ks[17], (DEPTH, P_EXPERTS, D_MODEL), D_MODEL ** -0.5),
        "peer_expert_v": nrm(ks[18], (DEPTH, P_EXPERTS, D_MODEL), P_HEADS ** -0.5),
    }


def reference(x_prompt, x_sample, cache_attn_k, cache_attn_v, state_retention,
              mix_norm_g, w_in, attn_q_norm_g, attn_k_norm_g, attn_rel_bias, ret_out_norm_g,
              w_branch_attn, w_branch_ret, w_out, ffn_norm_g, peer_w_query, peer_sub_keys,
              peer_expert_u, peer_expert_v):
    S = x_prompt.shape[1]
    T = x_sample.shape[1]
    pos_p = jnp.arange(S)
    pos_s = PAST_LEN + jnp.arange(T)
    keep_p = min(BAND_PAST, S)
    y_p, y_s = x_prompt, x_sample
    kp_l, vp_l, sp_l, ks_l, vs_l, ss_l = [], [], [], [], [], []
    for l in range(DEPTH):
        p = (mix_norm_g[l], w_in[l], attn_q_norm_g[l], attn_k_norm_g[l], attn_rel_bias[l],
             ret_out_norm_g[l], w_branch_attn[l], w_branch_ret[l], w_out[l], ffn_norm_g[l],
             peer_w_query[l], peer_sub_keys[l], peer_expert_u[l], peer_expert_v[l])

        def attend_prompt(q, k, v, rel_b):
            o = attention_prompt(q, k, v, rel_b)
            return o, k[:, S - keep_p:], v[:, S - keep_p:]

        def retain_prompt(q, k, v):
            return retention_prompt(q, k, v)

        ck, cv, cs = cache_attn_k[l], cache_attn_v[l], state_retention[l]

        def attend_sample(q, k, v, rel_b, ck=ck, cv=cv):
            r = ck.shape[1]
            kk = jnp.concatenate([ck, k], axis=1)
            vv = jnp.concatenate([cv, v], axis=1)
            qpos = jnp.arange(T)
            kpos = jnp.concatenate([jnp.arange(r) - r, jnp.arange(T)])
            kvalid = jnp.ones((r + T,), dtype=bool)
            return band_attention(q, kk, vv, qpos, kpos, kvalid, rel_b), k, v

        def retain_sample(q, k, v, cs=cs):
            return retention_block(q, k, v, cs)

        y_p, kp, vp, sp = layer_forward(y_p, pos_p, attend_prompt, retain_prompt, p)
        y_s, kn, vn, sn = layer_forward(y_s, pos_s, attend_sample, retain_sample, p)
        kp_l.append(kp); vp_l.append(vp); sp_l.append(sp)
        ks_l.append(kn); vs_l.append(vn); ss_l.append(sn)
    return (y_p, y_s, jnp.stack(kp_l), jnp.stack(vp_l), jnp.stack(sp_l),
            jnp.stack(ks_l), jnp.stack(vs_l), jnp.stack(ss_l))
```

```python
import functools

import numpy as np
import jax
import jax.numpy as jnp
from jax import lax
from jax.experimental import pallas as pl
from jax.experimental.pallas import tpu as pltpu

PAST_LEN = 1024
CHUNK = 64
EPS = 1e-6
A_HEADS = 8
A_HEAD_DIM = 64
A_WIDTH = A_HEADS * A_HEAD_DIM
BAND_CHUNKS = 8
BAND_PAST = BAND_CHUNKS * CHUNK
REL_CLIP = 128
MASK_VALUE = -1e30
R_HEADS = 4
R_KEY_DIM = 128
R_WIDTH = R_HEADS * R_KEY_DIM
ROPE_BASE = 10000.0
P_HEADS = 8
P_KEYS = 128
P_HALF = 128
P_TOPK = 16
P_SLOTS = P_HEADS * P_TOPK

LANES = 128
VMEM_LIMIT = 56 * 1024 * 1024

_NT = (((1,), (1,)), ((), ()))
_F32 = jnp.float32
_BF16 = jnp.bfloat16


def _params(n_axes, vmem=VMEM_LIMIT):
    return pltpu.CompilerParams(
        dimension_semantics=("arbitrary",) * n_axes, vmem_limit_bytes=vmem)


def _const_spec(shape):
    nd = len(shape)
    return pl.BlockSpec(shape, lambda *_: (0,) * nd)


def _rotate_heads(x, cs, sn):
    outs = []
    for h in range(R_HEADS):
        xh = x[:, h * R_KEY_DIM:(h + 1) * R_KEY_DIM]
        outs.append(xh * cs + pltpu.roll(xh, R_KEY_DIM // 2, axis=1) * sn)
    return jnp.concatenate(outs, axis=1)


def _in_proj_kernel(x_ref, g_ref, w_ref, hsum_ref, qg_ref, kg_ref, cs_ref, sn_ref,
                    qa_ref, ka_ref, va_ref, qr_ref, kr_ref, vr_ref, gr_ref, ga_ref, gt_ref):
    x = x_ref[...]
    h = x * lax.rsqrt(jnp.mean(x * x, axis=-1, keepdims=True) + EPS) * g_ref[...]
    hb = h.astype(_BF16)

    def seg(start, width):
        return jnp.dot(hb, w_ref[:, start:start + width], preferred_element_type=_F32)

    def head_norm(z, gain):
        ss = jnp.dot((z * z).astype(_BF16), hsum_ref[...], preferred_element_type=_F32)
        return z * lax.rsqrt(ss * (1.0 / A_HEAD_DIM) + EPS) * gain

    qa_ref[...] = head_norm(seg(0, A_WIDTH), qg_ref[...]).astype(_BF16)
    ka_ref[...] = head_norm(seg(A_WIDTH, A_WIDTH), kg_ref[...])
    va_ref[...] = seg(2 * A_WIDTH, A_WIDTH)
    o = 3 * A_WIDTH
    cs = cs_ref[...]
    sn = sn_ref[...]
    qr_ref[...] = _rotate_heads(seg(o, R_WIDTH), cs, sn).astype(_BF16)
    kr_ref[...] = (_rotate_heads(seg(o + R_WIDTH, R_WIDTH), cs, sn)
                   * (R_KEY_DIM ** -0.5)).astype(_BF16)
    vr_ref[...] = seg(o + 2 * R_WIDTH, R_WIDTH).astype(_BF16)
    gr_ref[...] = seg(o + 3 * R_WIDTH, R_WIDTH)
    o += 4 * R_WIDTH
    d = x.shape[-1]
    ga_ref[...] = seg(o, d)
    gt_ref[...] = seg(o + d, d)


def _in_proj(x2, mix_g, w_in_b, hsum, qg, kg, cs, sn, tm):
    n, d = x2.shape
    period = cs.shape[0] // tm
    row = lambda i: (i, 0)
    pos = lambda i: (i % period, 0)
    wide = lambda w, dt: jax.ShapeDtypeStruct((n, w), dt)
    outs = [wide(A_WIDTH, _BF16), wide(A_WIDTH, _F32), wide(A_WIDTH, _F32),
            wide(R_WIDTH, _BF16), wide(R_WIDTH, _BF16), wide(R_WIDTH, _BF16),
            wide(R_WIDTH, _F32), wide(d, _F32), wide(d, _F32)]
    return pl.pallas_call(
        _in_proj_kernel,
        grid=(n // tm,),
        in_specs=[pl.BlockSpec((tm, d), row), _const_spec(mix_g.shape),
                  _const_spec(w_in_b.shape), _const_spec(hsum.shape),
                  _const_spec(qg.shape), _const_spec(kg.shape),
                  pl.BlockSpec((tm, LANES), pos), pl.BlockSpec((tm, LANES), pos)],
        out_specs=[pl.BlockSpec((tm, o.shape[1]), row) for o in outs],
        out_shape=outs,
        compiler_params=_params(1),
        name="in_proj",
    )(x2, mix_g, w_in_b, hsum, qg, kg, cs, sn)


def _band_attn_kernel(*refs, n_slots, mask_leading):
    q_ref = refs[0]
    k_refs = refs[1:1 + n_slots]
    v_refs = refs[1 + n_slots:1 + 2 * n_slots]
    bias_ref = refs[1 + 2 * n_slots]
    o_ref = refs[2 + 2 * n_slots]
    i = pl.program_id(1)
    tq = q_ref.shape[1]
    low = lax.broadcasted_iota(jnp.int32, (1, LANES), 1) < A_HEAD_DIM
    offs = []
    for s in range(n_slots):
        if mask_leading and s < n_slots - 1:
            offs.append(jnp.where(i >= n_slots - 1 - s, 0.0, MASK_VALUE).astype(_F32))
        else:
            offs.append(None)
    for hp in range(A_HEADS // 2):
        cols = slice(hp * LANES, (hp + 1) * LANES)
        q2 = q_ref[0, :, cols] * (A_HEAD_DIM ** -0.5)
        zero = jnp.zeros_like(q2)
        ks = [k_refs[s][0, :, cols].astype(_BF16) for s in range(n_slots)]
        vs = [v_refs[s][0, :, cols].astype(_BF16) for s in range(n_slots)]
        halves = []
        for half in range(2):
            qh = jnp.where(low, q2, zero) if half == 0 else jnp.where(low, zero, q2)
            parts = []
            for s in range(n_slots):
                sc = lax.dot_general(qh, ks[s], _NT, preferred_element_type=_F32)
                if offs[s] is not None:
                    sc = sc + offs[s]
                parts.append(sc)
            sc = jnp.concatenate(parts, axis=1) + bias_ref[2 * hp + half]
            m = jnp.max(sc, axis=-1, keepdims=True)
            p = jnp.exp(sc - m)
            l = jnp.sum(p, axis=-1, keepdims=True)
            pb = p.astype(_BF16)
            acc = jnp.zeros((tq, LANES), _F32)
            c0 = 0
            for s in range(n_slots):
                w = ks[s].shape[0]
                acc = acc + jnp.dot(pb[:, c0:c0 + w], vs[s], preferred_element_type=_F32)
                c0 += w
            halves.append(acc / l)
        o_ref[0, :, cols] = jnp.where(low, halves[0], halves[1]).astype(_BF16)


def _band_attn(q, k_slots, v_slots, bias, tq, mask_leading):
    b, t, w = q.shape
    n_slots = len(k_slots)
    specs = [pl.BlockSpec((1, tq, w), lambda bi, i: (bi, i, 0))]
    args = [q]
    for arr, rows, imap in list(k_slots) + list(v_slots):
        specs.append(pl.BlockSpec((1, rows, w), imap))
        args.append(arr)
    specs.append(_const_spec(bias.shape))
    args.append(bias)
    return pl.pallas_call(
        functools.partial(_band_attn_kernel, n_slots=n_slots, mask_leading=mask_leading),
        grid=(b, t // tq),
        in_specs=specs,
        out_specs=pl.BlockSpec((1, tq, w), lambda bi, i: (bi, i, 0)),
        out_shape=jax.ShapeDtypeStruct((b, t, w), _BF16),
        compiler_params=_params(2),
        name="band_attn",
    )(*args)


def _band_bias(rel_bias, tq, tk):
    i = np.arange(tq)[:, None]
    j = np.arange(tk)[None, :]
    rel = np.clip(i + (tk - tq) - j, -REL_CLIP, REL_CLIP) + REL_CLIP
    qc = i // CHUNK
    kc = j // CHUNK
    valid = (kc >= qc) & (kc <= qc + BAND_CHUNKS)
    bias = rel_bias.astype(_F32)[:, rel]
    return jnp.where(jnp.asarray(valid)[None], bias, MASK_VALUE)


def _retention_kernel(q_ref, k_ref, v_ref, gr_ref, s0_ref, dec_ref, qd_ref, kd_ref, sd_ref,
                      rg_ref, o_ref, s_ref):
    @pl.when(pl.program_id(1) == 0)
    def _():
        s_ref[...] = s0_ref[...]

    for h in range(R_HEADS):
        cols = slice(h * R_KEY_DIM, (h + 1) * R_KEY_DIM)
        q = q_ref[0, :, cols]
        k = k_ref[0, :, cols]
        v = v_ref[0, :, cols]
        state = s_ref[0, h]
        inner = lax.dot_general(q, k, _NT, preferred_element_type=_F32) * dec_ref[h]
        o = jnp.dot(inner.astype(_BF16), v, preferred_element_type=_F32)
        o = o + jnp.dot(q, state.astype(_BF16), preferred_element_type=_F32) * qd_ref[:, cols]
        kt = (k.astype(_F32) * kd_ref[:, cols]).T.astype(_BF16)
        s_ref[0, h] = state * sd_ref[:, cols] + jnp.dot(kt, v, preferred_element_type=_F32)
        on = o * lax.rsqrt(jnp.mean(o * o, axis=-1, keepdims=True) + EPS) * rg_ref[:, cols]
        g = gr_ref[0, :, cols]
        o_ref[0, :, cols] = (on * (g * jax.nn.sigmoid(g))).astype(_BF16)


def _retention_tables(length):
    log_gamma = jnp.log1p(-(2.0 ** (-5.0 - jnp.arange(R_HEADS, dtype=_F32))))
    idx = jnp.arange(length, dtype=_F32)
    diff = idx[:, None] - idx[None, :]
    decay = jnp.where(diff >= 0, jnp.exp(log_gamma[:, None, None] * jnp.maximum(diff, 0.0)), 0.0)
    wide = lambda a: jnp.repeat(a, R_KEY_DIM, axis=-1)
    q_decay = wide(jnp.exp(log_gamma[None, :] * (idx[:, None] + 1.0)))
    k_decay = wide(jnp.exp(log_gamma[None, :] * (length - 1.0 - idx)[:, None]))
    s_decay = wide(jnp.exp(log_gamma * length)[None, :])
    return decay, q_decay, k_decay, s_decay


def _retention(qr, kr, vr, gr, s0, r_gain, length):
    b, t, w = qr.shape
    decay, q_decay, k_decay, s_decay = _retention_tables(length)
    blk = pl.BlockSpec((1, length, w), lambda bi, i: (bi, i, 0))
    st = pl.BlockSpec((1,) + s0.shape[1:], lambda bi, i: (bi, 0, 0, 0))
    return pl.pallas_call(
        _retention_kernel,
        grid=(b, t // length),
        in_specs=[blk, blk, blk, blk, st, _const_spec(decay.shape), _const_spec(q_decay.shape),
                  _const_spec(k_decay.shape), _const_spec(s_decay.shape),
                  _const_spec(r_gain.shape)],
        out_specs=[blk, st],
        out_shape=[jax.ShapeDtypeStruct((b, t, w), _BF16),
                   jax.ShapeDtypeStruct(s0.shape, _F32)],
        compiler_params=_params(2),
        name="retention",
    )(qr, kr, vr, gr, s0, decay, q_decay, k_decay, s_decay, r_gain)


def _merge_kernel(x_ref, oa_ref, og_ref, ga_ref, gt_ref, wa_ref, wr_ref, wo_ref, fg_ref, wq_ref,
                  x1_ref, h2_ref, qry_ref):
    br_a = jnp.dot(oa_ref[...], wa_ref[...], preferred_element_type=_F32)
    br_r = jnp.dot(og_ref[...], wr_ref[...], preferred_element_type=_F32)
    merged = jax.nn.sigmoid(ga_ref[...]) * br_a + jax.nn.sigmoid(gt_ref[...]) * br_r
    x1 = x_ref[...] + jnp.dot(merged.astype(_BF16), wo_ref[...], preferred_element_type=_F32)
    x1_ref[...] = x1
    h2 = (x1 * lax.rsqrt(jnp.mean(x1 * x1, axis=-1, keepdims=True) + EPS)
          * fg_ref[...]).astype(_BF16)
    h2_ref[...] = h2
    qry_ref[...] = jnp.dot(h2, wq_ref[...], preferred_element_type=_F32).astype(_BF16)


def _merge(x2, oa, og, ga, gt, wa, wr, wo, fg, wq, tm):
    n, d = x2.shape
    row = lambda i: (i, 0)
    blk = lambda a: pl.BlockSpec((tm, a.shape[1]), row)
    outs = [jax.ShapeDtypeStruct((n, d), _F32), jax.ShapeDtypeStruct((n, d), _BF16),
            jax.ShapeDtypeStruct((n, wq.shape[1]), _BF16)]
    return pl.pallas_call(
        _merge_kernel,
        grid=(n // tm,),
        in_specs=[blk(x2), blk(oa), blk(og), blk(ga), blk(gt), _const_spec(wa.shape),
                  _const_spec(wr.shape), _const_spec(wo.shape), _const_spec(fg.shape),
                  _const_spec(wq.shape)],
        out_specs=[pl.BlockSpec((tm, o.shape[1]), row) for o in outs],
        out_shape=outs,
        compiler_params=_params(1),
        name="merge",
    )(x2, oa, og, ga, gt, wa, wr, wo, fg, wq)


def _top_rows(s, k):
    rows = lax.broadcasted_iota(jnp.int32, s.shape, 0)
    vals, idxs = [], []
    for _ in range(k):
        m = jnp.max(s, axis=0, keepdims=True)
        am = jnp.min(jnp.where(s == m, rows, s.shape[0]), axis=0, keepdims=True)
        s = jnp.where(rows == am, -jnp.inf, s)
        vals.append(m)
        idxs.append(am)
    return vals, idxs


def _stack_rows(rows_list):
    k = len(rows_list)
    rid = lax.broadcasted_iota(jnp.int32, (k, rows_list[0].shape[1]), 0)
    out = jnp.broadcast_to(rows_list[0], rid.shape)
    for r in range(1, k):
        out = jnp.where(rid == r, rows_list[r], out)
    return out


def _peer_route_kernel(qry_ref, keys_ref, idx_ref, gate_ref):
    experts, gates = [], []
    for h in range(P_HEADS):
        tops = []
        for c in range(2):
            qc = qry_ref[:, (2 * h + c) * P_HALF:(2 * h + c + 1) * P_HALF]
            s = lax.dot_general(keys_ref[2 * h + c], qc, _NT, preferred_element_type=_F32)
            tops.append(_top_rows(s, P_TOPK))
        (v1, i1), (v2, i2) = tops
        s2 = _stack_rows(v2)
        cand = jnp.concatenate([v1[a] + s2 for a in range(P_TOPK)], axis=0)
        best, pos = _top_rows(cand, P_TOPK)
        best = _stack_rows(best)
        pos = _stack_rows(pos)
        pa = lax.shift_right_logical(pos, 4)
        pb = jnp.bitwise_and(pos, P_TOPK - 1)
        e1 = jnp.zeros_like(pos)
        e2 = jnp.zeros_like(pos)
        for a in range(P_TOPK):
            e1 = jnp.where(pa == a, i1[a], e1)
            e2 = jnp.where(pb == a, i2[a], e2)
        experts.append(e1 * P_KEYS + e2)
        ex = jnp.exp(best - best[0:1])
        gates.append(ex / jnp.sum(ex, axis=0, keepdims=True))
    idx_ref[...] = jnp.concatenate(experts, axis=0).T
    gate_ref[...] = jnp.concatenate(gates, axis=0).T


def _peer_route(qry, keys_b, tm):
    n, w = qry.shape
    row = lambda i: (i, 0)
    return pl.pallas_call(
        _peer_route_kernel,
        grid=(n // tm,),
        in_specs=[pl.BlockSpec((tm, w), row), _const_spec(keys_b.shape)],
        out_specs=[pl.BlockSpec((tm, P_SLOTS), row), pl.BlockSpec((tm, P_SLOTS), row)],
        out_shape=[jax.ShapeDtypeStruct((n, P_SLOTS), jnp.int32),
                   jax.ShapeDtypeStruct((n, P_SLOTS), _F32)],
        compiler_params=_params(1),
        name="peer_route",
    )(qry, keys_b)


EXPERT_ROWS = 4


def _pack_table(table):
    e, d = table.shape
    bits = lax.bitcast_convert_type(table.astype(_BF16), jnp.uint16).astype(jnp.uint32)
    packed = bits[:, :d // 2] | (bits[:, d // 2:] << 16)
    return lax.bitcast_convert_type(packed, jnp.int32).reshape(e, EXPERT_ROWS, LANES)


def _unpack_row(words):
    lo = lax.bitcast_convert_type(words << 16, _F32)
    hi = lax.bitcast_convert_type(words & jnp.int32(-65536), _F32)
    return lo, hi


def _load_table_once(tbl_hbm, tbl_ref, sem):
    @pl.when(pl.program_id(0) == 0)
    def _():
        cp = pltpu.make_async_copy(tbl_hbm, tbl_ref, sem)
        cp.start()
        cp.wait()


def _gelu(a):
    return 0.5 * a * (1.0 + lax.erf(a * (2.0 ** -0.5)))


def _peer_up_kernel(idx_ref, h_ref, gate_ref, sel_ref, tbl_hbm, act_ref, tbl_ref, prod_ref, a_ref,
                    sem):
    _load_table_once(tbl_hbm, tbl_ref, sem)
    tm = h_ref.shape[0]
    ones = jnp.ones((8, LANES), _F32)

    def token(t, carry):
        h = h_ref[t]
        h_lo = h[0:EXPERT_ROWS]
        h_hi = h[EXPERT_ROWS:2 * EXPERT_ROWS]
        for j in range(P_SLOTS):
            lo, hi = _unpack_row(tbl_ref[idx_ref[t, j]])
            prod_ref[j * EXPERT_ROWS:(j + 1) * EXPERT_ROWS, :] = lo * h_lo + hi * h_hi
        part = jnp.dot(sel_ref[...], prod_ref[...].astype(_BF16), preferred_element_type=_F32)
        row = lax.dot_general(ones, part, _NT, preferred_element_type=_F32,
                              precision=lax.Precision.HIGHEST)
        a_ref[pl.ds(t, 1), :] = row[0:1]
        return carry

    lax.fori_loop(0, tm, token, 0)
    act_ref[...] = _gelu(a_ref[...]) * gate_ref[...]


def _peer_up(idx, h3, gate, table, tm):
    n = idx.shape[0]
    sel = np.zeros((P_SLOTS, P_SLOTS * EXPERT_ROWS), np.float32)
    sel[np.arange(P_SLOTS * EXPERT_ROWS) // EXPERT_ROWS, np.arange(P_SLOTS * EXPERT_ROWS)] = 1.0
    sel = jnp.asarray(sel, _BF16)
    row = lambda i: (i, 0)
    return pl.pallas_call(
        _peer_up_kernel,
        grid=(n // tm,),
        in_specs=[pl.BlockSpec((tm, P_SLOTS), row, memory_space=pltpu.SMEM),
                  pl.BlockSpec((tm, 8, LANES), lambda i: (i, 0, 0)),
                  pl.BlockSpec((tm, P_SLOTS), row), _const_spec(sel.shape),
                  pl.BlockSpec(memory_space=pl.ANY)],
        out_specs=pl.BlockSpec((tm, P_SLOTS), row),
        out_shape=jax.ShapeDtypeStruct((n, P_SLOTS), _F32),
        scratch_shapes=[pltpu.VMEM(table.shape, jnp.int32),
                        pltpu.VMEM((P_SLOTS * EXPERT_ROWS, LANES), _F32),
                        pltpu.VMEM((tm, P_SLOTS), _F32),
                        pltpu.SemaphoreType.DMA(())],
        compiler_params=_params(1),
        name="peer_up",
    )(idx, h3, gate, sel, table)


def _peer_down_kernel(idx_ref, act_ref, x_ref, tbl_hbm, y_ref, tbl_ref, sem):
    _load_table_once(tbl_hbm, tbl_ref, sem)
    tm = x_ref.shape[0]
    n_acc = 4

    def token(t, carry):
        acc_lo = [jnp.zeros((EXPERT_ROWS, LANES), _F32) for _ in range(n_acc)]
        acc_hi = [jnp.zeros((EXPERT_ROWS, LANES), _F32) for _ in range(n_acc)]
        for j in range(P_SLOTS):
            lo, hi = _unpack_row(tbl_ref[idx_ref[t, j]])
            w = act_ref[t, j]
            acc_lo[j % n_acc] = acc_lo[j % n_acc] + w * lo
            acc_hi[j % n_acc] = acc_hi[j % n_acc] + w * hi
        x = x_ref[t]
        y_ref[t, 0:EXPERT_ROWS, :] = x[0:EXPERT_ROWS] + sum(acc_lo[1:], acc_lo[0])
        y_ref[t, EXPERT_ROWS:2 * EXPERT_ROWS, :] = (x[EXPERT_ROWS:2 * EXPERT_ROWS]
                                                    + sum(acc_hi[1:], acc_hi[0]))
        return carry

    lax.fori_loop(0, tm, token, 0)


def _peer_down(idx, act, x3, table, tm):
    n = idx.shape[0]
    row = lambda i: (i, 0)
    tok = pl.BlockSpec((tm, 8, LANES), lambda i: (i, 0, 0))
    return pl.pallas_call(
        _peer_down_kernel,
        grid=(n // tm,),
        in_specs=[pl.BlockSpec((tm, P_SLOTS), row, memory_space=pltpu.SMEM),
                  pl.BlockSpec((tm, P_SLOTS), row, memory_space=pltpu.SMEM),
                  tok, pl.BlockSpec(memory_space=pl.ANY)],
        out_specs=tok,
        out_shape=jax.ShapeDtypeStruct(x3.shape, _F32),
        scratch_shapes=[pltpu.VMEM(table.shape, jnp.int32), pltpu.SemaphoreType.DMA(())],
        compiler_params=_params(1),
        name="peer_down",
    )(idx, act, x3, table)


def _rope_tables(pos):
    half = R_KEY_DIM // 2
    inv = ROPE_BASE ** (-jnp.arange(half, dtype=_F32) / half)
    ang = pos.astype(_F32)[:, None] * inv[None, :]
    cos, sin = jnp.cos(ang), jnp.sin(ang)
    return jnp.concatenate([cos, cos], axis=1), jnp.concatenate([-sin, sin], axis=1)


def _pick_tile(n, want):
    while n % want:
        want //= 2
    return want


def _layer(x, pos, weights, attend, s0, ret_len):
    (mix_g, w_in_b, hsum, qg, kg, r_gain, wa, wr, wo, fg, wq, keys_b, tbl_u, tbl_v) = weights
    b, t, d = x.shape
    n = b * t
    x2 = x.reshape(n, d)
    tm = _pick_tile(n, 512)
    cs, sn = _rope_tables(jnp.tile(pos, max(tm // t, 1)))
    qa, ka, va, qr, kr, vr, gr, ga, gt = _in_proj(x2, mix_g, w_in_b, hsum, qg, kg, cs, sn, tm)
    shp = lambda a: a.reshape(b, t, a.shape[1])
    oa = attend(shp(qa), shp(ka), shp(va))
    og, s_new = _retention(shp(qr), shp(kr), shp(vr), shp(gr), s0, r_gain, ret_len)
    x1, h2, qry = _merge(x2, oa.reshape(n, -1), og.reshape(n, -1), ga, gt, wa, wr, wo, fg, wq, tm)
    tr = _pick_tile(n, 256)
    idx, gate = _peer_route(qry, keys_b, tr)
    tp = _pick_tile(n, 128)
    h3 = h2.astype(_F32).reshape(n, 8, LANES)
    act = _peer_up(idx, h3, gate, tbl_u, tp)
    y = _peer_down(idx, act, x1.reshape(n, 8, LANES), tbl_v, tp)
    return y.reshape(b, t, d), shp(ka), shp(va), s_new


def kernel(x_prompt, x_sample, cache_attn_k, cache_attn_v, state_retention, mix_norm_g, w_in, attn_q_norm_g, attn_k_norm_g, attn_rel_bias, ret_out_norm_g, w_branch_attn, w_branch_ret, w_out, ffn_norm_g, peer_w_query, peer_sub_keys, peer_expert_u, peer_expert_v):
    depth = w_in.shape[0]
    assert depth == 1, "single-layer step"
    bp, s, d = x_prompt.shape
    bs, t, _ = x_sample.shape
    l = 0
    hsum = jnp.asarray(np.kron(np.eye(A_HEADS), np.ones((A_HEAD_DIM, A_HEAD_DIM))), _BF16)
    weights = (
        mix_norm_g[l][None, :], w_in[l].astype(_BF16), hsum,
        jnp.tile(attn_q_norm_g[l], A_HEADS)[None, :], jnp.tile(attn_k_norm_g[l], A_HEADS)[None, :],
        ret_out_norm_g[l].reshape(1, R_WIDTH),
        w_branch_attn[l].astype(_BF16), w_branch_ret[l].astype(_BF16), w_out[l].astype(_BF16),
        ffn_norm_g[l][None, :], peer_w_query[l].astype(_BF16),
        peer_sub_keys[l].reshape(2 * P_HEADS, P_KEYS, P_HALF).astype(_BF16),
        _pack_table(peer_expert_u[l]), _pack_table(peer_expert_v[l]),
    )
    rel_bias = attn_rel_bias[l]

    tq = 256
    n_past = BAND_PAST // tq
    bias_p = _band_bias(rel_bias, tq, BAND_PAST + tq)

    def attend_prompt(q, k, v):
        def slots(a):
            return [(a, tq, functools.partial(
                lambda bi, i, back: (bi, jnp.maximum(i - back, 0), 0), back=n_past - sl))
                for sl in range(n_past + 1)]
        return _band_attn(q, slots(k), slots(v), bias_p, tq, mask_leading=True)

    rows = cache_attn_k.shape[2]
    ck = cache_attn_k[l].reshape(bs, rows, A_WIDTH)
    cv = cache_attn_v[l].reshape(bs, rows, A_WIDTH)
    bias_s = _band_bias(rel_bias, t, rows + t)

    def attend_sample(q, k, v):
        zero = lambda bi, i: (bi, 0, 0)
        return _band_attn(q, [(ck, rows, zero), (k, t, zero)], [(cv, rows, zero), (v, t, zero)],
                          bias_s, t, mask_leading=False)

    s0_p = jnp.zeros((bp, R_HEADS, R_KEY_DIM, R_KEY_DIM), _F32)
    y_p, k_p, v_p, s_p = _layer(x_prompt, jnp.arange(s), weights, attend_prompt, s0_p, 256)
    y_s, k_s, v_s, s_s = _layer(x_sample, PAST_LEN + jnp.arange(t), weights, attend_sample,
                                state_retention[l], t)
    keep = min(BAND_PAST, s)
    heads = lambda a: a.reshape(a.shape[0], a.shape[1], A_HEADS, A_HEAD_DIM)[None]
    return (y_p, y_s, heads(k_p[:, s - keep:]), heads(v_p[:, s - keep:]), s_p[None],
            heads(k_s), heads(v_s), s_s[None])
```

```python
import functools

import numpy as np
import jax
import jax.numpy as jnp
from jax import lax
from jax.experimental import pallas as pl
from jax.experimental.pallas import tpu as pltpu

PAST_LEN = 1024
CHUNK = 64
EPS = 1e-6
A_HEADS = 8
A_HEAD_DIM = 64
A_WIDTH = A_HEADS * A_HEAD_DIM
BAND_CHUNKS = 8
BAND_PAST = BAND_CHUNKS * CHUNK
REL_CLIP = 128
MASK_VALUE = -1e30
R_HEADS = 4
R_KEY_DIM = 128
R_WIDTH = R_HEADS * R_KEY_DIM
ROPE_BASE = 10000.0
P_HEADS = 8
P_KEYS = 128
P_HALF = 128
P_TOPK = 16
P_SLOTS = P_HEADS * P_TOPK

LANES = 128
SUBLANES = 8
VMEM_LIMIT = 56 * 1024 * 1024

EXPERT_ROWS = 4
TOKEN_GROUP = 8
GROUP_ROWS = 2 * SUBLANES

_NT = (((1,), (1,)), ((), ()))
_F32 = jnp.float32
_BF16 = jnp.bfloat16


def _params(n_axes, vmem=VMEM_LIMIT):
    return pltpu.CompilerParams(
        dimension_semantics=("arbitrary",) * n_axes, vmem_limit_bytes=vmem)


def _const_spec(shape):
    nd = len(shape)
    return pl.BlockSpec(shape, lambda *_: (0,) * nd)


def _rotate_heads(x, cs, sn):
    outs = []
    for h in range(R_HEADS):
        xh = x[:, h * R_KEY_DIM:(h + 1) * R_KEY_DIM]
        outs.append(xh * cs + pltpu.roll(xh, R_KEY_DIM // 2, axis=1) * sn)
    return jnp.concatenate(outs, axis=1)


def _in_proj_kernel(x_ref, g_ref, w_ref, hsum_ref, qg_ref, kg_ref, cs_ref, sn_ref,
                    qa_ref, ka_ref, va_ref, qr_ref, kr_ref, vr_ref, gr_ref, ga_ref, gt_ref):
    x = x_ref[...]
    h = x * lax.rsqrt(jnp.mean(x * x, axis=-1, keepdims=True) + EPS) * g_ref[...]
    hb = h.astype(_BF16)

    def seg(start, width):
        return jnp.dot(hb, w_ref[:, start:start + width], preferred_element_type=_F32)

    def head_norm(z, gain):
        ss = jnp.dot((z * z).astype(_BF16), hsum_ref[...], preferred_element_type=_F32)
        return z * lax.rsqrt(ss * (1.0 / A_HEAD_DIM) + EPS) * gain

    qa_ref[...] = head_norm(seg(0, A_WIDTH), qg_ref[...]).astype(_BF16)
    ka_ref[...] = head_norm(seg(A_WIDTH, A_WIDTH), kg_ref[...])
    va_ref[...] = seg(2 * A_WIDTH, A_WIDTH)
    o = 3 * A_WIDTH
    cs = cs_ref[...]
    sn = sn_ref[...]
    qr_ref[...] = _rotate_heads(seg(o, R_WIDTH), cs, sn).astype(_BF16)
    kr_ref[...] = (_rotate_heads(seg(o + R_WIDTH, R_WIDTH), cs, sn)
                   * (R_KEY_DIM ** -0.5)).astype(_BF16)
    vr_ref[...] = seg(o + 2 * R_WIDTH, R_WIDTH).astype(_BF16)
    gr_ref[...] = seg(o + 3 * R_WIDTH, R_WIDTH)
    o += 4 * R_WIDTH
    d = x.shape[-1]
    ga_ref[...] = seg(o, d)
    gt_ref[...] = seg(o + d, d)


def _in_proj(x2, mix_g, w_in_b, hsum, qg, kg, cs, sn, tm):
    n, d = x2.shape
    period = cs.shape[0] // tm
    row = lambda i: (i, 0)
    pos = lambda i: (i % period, 0)
    wide = lambda w, dt: jax.ShapeDtypeStruct((n, w), dt)
    outs = [wide(A_WIDTH, _BF16), wide(A_WIDTH, _F32), wide(A_WIDTH, _F32),
            wide(R_WIDTH, _BF16), wide(R_WIDTH, _BF16), wide(R_WIDTH, _BF16),
            wide(R_WIDTH, _F32), wide(d, _F32), wide(d, _F32)]
    return pl.pallas_call(
        _in_proj_kernel,
        grid=(n // tm,),
        in_specs=[pl.BlockSpec((tm, d), row), _const_spec(mix_g.shape),
                  _const_spec(w_in_b.shape), _const_spec(hsum.shape),
                  _const_spec(qg.shape), _const_spec(kg.shape),
                  pl.BlockSpec((tm, LANES), pos), pl.BlockSpec((tm, LANES), pos)],
        out_specs=[pl.BlockSpec((tm, o.shape[1]), row) for o in outs],
        out_shape=outs,
        compiler_params=_params(1),
        name="in_proj",
    )(x2, mix_g, w_in_b, hsum, qg, kg, cs, sn)


def _band_attn_kernel(*refs, n_slots, mask_leading):
    q_ref = refs[0]
    k_refs = refs[1:1 + n_slots]
    v_refs = refs[1 + n_slots:1 + 2 * n_slots]
    rel_ref = refs[1 + 2 * n_slots]
    o_ref = refs[2 + 2 * n_slots]
    bias_ref = refs[3 + 2 * n_slots]
    i = pl.program_id(1)
    tq = q_ref.shape[1]
    tk = bias_ref.shape[2]

    @pl.when((pl.program_id(0) == 0) & (i == 0))
    def _():
        qc = lax.shift_right_logical(lax.broadcasted_iota(jnp.int32, (tq, tk), 0), 6)
        kc = lax.shift_right_logical(lax.broadcasted_iota(jnp.int32, (tq, tk), 1), 6)
        valid = (kc >= qc) & (kc <= qc + BAND_CHUNKS)
        for h in range(A_HEADS):
            row = jnp.broadcast_to(rel_ref[h:h + 1, :], (tq, rel_ref.shape[1]))
            toep = pltpu.roll(row, tk, 1, stride=1, stride_axis=0)
            bias_ref[h] = jnp.where(valid, toep[:, :tk], MASK_VALUE)

    low = lax.broadcasted_iota(jnp.int32, (1, LANES), 1) < A_HEAD_DIM
    offs = []
    for s in range(n_slots):
        if mask_leading and s < n_slots - 1:
            offs.append(jnp.where(i >= n_slots - 1 - s, 0.0, MASK_VALUE).astype(_F32))
        else:
            offs.append(None)
    for hp in range(A_HEADS // 2):
        cols = slice(hp * LANES, (hp + 1) * LANES)
        q2 = q_ref[0, :, cols] * (A_HEAD_DIM ** -0.5)
        zero = jnp.zeros_like(q2)
        ks = [k_refs[s][0, :, cols].astype(_BF16) for s in range(n_slots)]
        vs = [v_refs[s][0, :, cols].astype(_BF16) for s in range(n_slots)]
        halves = []
        for half in range(2):
            qh = jnp.where(low, q2, zero) if half == 0 else jnp.where(low, zero, q2)
            parts = []
            for s in range(n_slots):
                sc = lax.dot_general(qh, ks[s], _NT, preferred_element_type=_F32)
                if offs[s] is not None:
                    sc = sc + offs[s]
                parts.append(sc)
            sc = jnp.concatenate(parts, axis=1) + bias_ref[2 * hp + half]
            m = jnp.max(sc, axis=-1, keepdims=True)
            p = jnp.exp(sc - m)
            l = jnp.sum(p, axis=-1, keepdims=True)
            pb = p.astype(_BF16)
            acc = jnp.zeros((tq, LANES), _F32)
            c0 = 0
            for s in range(n_slots):
                w = ks[s].shape[0]
                acc = acc + jnp.dot(pb[:, c0:c0 + w], vs[s], preferred_element_type=_F32)
                c0 += w
            halves.append(acc / l)
        o_ref[0, :, cols] = jnp.where(low, halves[0], halves[1]).astype(_BF16)


def _band_attn(q, k_slots, v_slots, rel_rev, tq, tk, mask_leading):
    b, t, w = q.shape
    n_slots = len(k_slots)
    specs = [pl.BlockSpec((1, tq, w), lambda bi, i: (bi, i, 0))]
    args = [q]
    for arr, rows, imap in list(k_slots) + list(v_slots):
        specs.append(pl.BlockSpec((1, rows, w), imap))
        args.append(arr)
    specs.append(_const_spec(rel_rev.shape))
    args.append(rel_rev)
    return pl.pallas_call(
        functools.partial(_band_attn_kernel, n_slots=n_slots, mask_leading=mask_leading),
        grid=(b, t // tq),
        in_specs=specs,
        out_specs=pl.BlockSpec((1, tq, w), lambda bi, i: (bi, i, 0)),
        out_shape=jax.ShapeDtypeStruct((b, t, w), _BF16),
        scratch_shapes=[pltpu.VMEM((A_HEADS, tq, tk), _F32)],
        compiler_params=_params(2),
        name="band_attn",
    )(*args)


def _rel_bias_row(rel_bias, tq, tk):
    n = -(-(tq + tk) // LANES) * LANES
    first = 1 + REL_CLIP - tq
    left = max(-first, 0)
    body = rel_bias.astype(_F32)[:, max(first, 0):]
    row = jnp.pad(body, ((0, 0), (left, n - left - body.shape[1])), mode="edge")
    return row[:, ::-1]


def _retention_kernel(q_ref, k_ref, v_ref, gr_ref, s0_ref, dec_ref, qd_ref, kd_ref, sd_ref,
                      rg_ref, o_ref, s_ref):
    @pl.when(pl.program_id(1) == 0)
    def _():
        s_ref[...] = s0_ref[...]

    for h in range(R_HEADS):
        cols = slice(h * R_KEY_DIM, (h + 1) * R_KEY_DIM)
        q = q_ref[0, :, cols]
        k = k_ref[0, :, cols]
        v = v_ref[0, :, cols]
        state = s_ref[0, h]
        inner = lax.dot_general(q, k, _NT, preferred_element_type=_F32) * dec_ref[h]
        o = jnp.dot(inner.astype(_BF16), v, preferred_element_type=_F32)
        o = o + jnp.dot(q, state.astype(_BF16), preferred_element_type=_F32) * qd_ref[:, cols]
        kt = (k.astype(_F32) * kd_ref[:, cols]).T.astype(_BF16)
        s_ref[0, h] = state * sd_ref[:, cols] + jnp.dot(kt, v, preferred_element_type=_F32)
        on = o * lax.rsqrt(jnp.mean(o * o, axis=-1, keepdims=True) + EPS) * rg_ref[:, cols]
        g = gr_ref[0, :, cols]
        o_ref[0, :, cols] = (on * (g * jax.nn.sigmoid(g))).astype(_BF16)


def _retention_tables(length):
    log_gamma = jnp.log1p(-(2.0 ** (-5.0 - jnp.arange(R_HEADS, dtype=_F32))))
    idx = jnp.arange(length, dtype=_F32)
    diff = idx[:, None] - idx[None, :]
    decay = jnp.where(diff >= 0, jnp.exp(log_gamma[:, None, None] * jnp.maximum(diff, 0.0)), 0.0)
    wide = lambda a: jnp.repeat(a, R_KEY_DIM, axis=-1)
    q_decay = wide(jnp.exp(log_gamma[None, :] * (idx[:, None] + 1.0)))
    k_decay = wide(jnp.exp(log_gamma[None, :] * (length - 1.0 - idx)[:, None]))
    s_decay = wide(jnp.exp(log_gamma * length)[None, :])
    return decay, q_decay, k_decay, s_decay


def _retention(qr, kr, vr, gr, s0, r_gain, length):
    b, t, w = qr.shape
    decay, q_decay, k_decay, s_decay = _retention_tables(length)
    blk = pl.BlockSpec((1, length, w), lambda bi, i: (bi, i, 0))
    st = pl.BlockSpec((1,) + s0.shape[1:], lambda bi, i: (bi, 0, 0, 0))
    return pl.pallas_call(
        _retention_kernel,
        grid=(b, t // length),
        in_specs=[blk, blk, blk, blk, st, _const_spec(decay.shape), _const_spec(q_decay.shape),
                  _const_spec(k_decay.shape), _const_spec(s_decay.shape),
                  _const_spec(r_gain.shape)],
        out_specs=[blk, st],
        out_shape=[jax.ShapeDtypeStruct((b, t, w), _BF16),
                   jax.ShapeDtypeStruct(s0.shape, _F32)],
        compiler_params=_params(2),
        name="retention",
    )(qr, kr, vr, gr, s0, decay, q_decay, k_decay, s_decay, r_gain)


def _merge_kernel(x_ref, oa_ref, og_ref, ga_ref, gt_ref, wa_ref, wr_ref, wo_ref, fg_ref, wq_ref,
                  x1_ref, h2_ref, qry_ref):
    br_a = jnp.dot(oa_ref[...], wa_ref[...], preferred_element_type=_F32)
    br_r = jnp.dot(og_ref[...], wr_ref[...], preferred_element_type=_F32)
    merged = jax.nn.sigmoid(ga_ref[...]) * br_a + jax.nn.sigmoid(gt_ref[...]) * br_r
    x1 = x_ref[...] + jnp.dot(merged.astype(_BF16), wo_ref[...], preferred_element_type=_F32)
    x1_ref[...] = x1
    h2 = (x1 * lax.rsqrt(jnp.mean(x1 * x1, axis=-1, keepdims=True) + EPS)
          * fg_ref[...]).astype(_BF16)
    h2_ref[...] = h2
    qry_ref[...] = jnp.dot(h2, wq_ref[...], preferred_element_type=_F32).astype(_BF16)


def _merge(x2, oa, og, ga, gt, wa, wr, wo, fg, wq, tm):
    n, d = x2.shape
    row = lambda i: (i, 0)
    blk = lambda a: pl.BlockSpec((tm, a.shape[1]), row)
    outs = [jax.ShapeDtypeStruct((n, d), _F32), jax.ShapeDtypeStruct((n, d), _BF16),
            jax.ShapeDtypeStruct((n, wq.shape[1]), _BF16)]
    return pl.pallas_call(
        _merge_kernel,
        grid=(n // tm,),
        in_specs=[blk(x2), blk(oa), blk(og), blk(ga), blk(gt), _const_spec(wa.shape),
                  _const_spec(wr.shape), _const_spec(wo.shape), _const_spec(fg.shape),
                  _const_spec(wq.shape)],
        out_specs=[pl.BlockSpec((tm, o.shape[1]), row) for o in outs],
        out_shape=outs,
        compiler_params=_params(1),
        name="merge",
    )(x2, oa, og, ga, gt, wa, wr, wo, fg, wq)


_NO_ROW = 1e9


def _top_rows(s, ids, k):
    vals, picked = [], []
    for _ in range(k):
        m = jnp.max(s, axis=0, keepdims=True)
        am = jnp.min(jnp.where(s == m, ids, _NO_ROW), axis=0, keepdims=True)
        s = jnp.where(ids == am, -jnp.inf, s)
        vals.append(m)
        picked.append(am)
    return vals, picked


def _stack_rows(rows_list):
    k = len(rows_list)
    rid = lax.broadcasted_iota(jnp.int32, (k, rows_list[0].shape[1]), 0)
    out = jnp.broadcast_to(rows_list[0], rid.shape)
    for r in range(1, k):
        out = jnp.where(rid == r, rows_list[r], out)
    return out


def _candidate_ids():
    ids = [b for b in range(P_TOPK)]
    for a in range(1, SUBLANES):
        ids += [a * P_TOPK + b for b in range(SUBLANES)]
    ids += [a * P_TOPK for a in range(SUBLANES, P_TOPK)]
    return np.asarray(ids, np.float32)


def _peer_route_kernel(qry_ref, keys_ref, cid_ref, idx_ref, gate_ref):
    tm = qry_ref.shape[0]
    key_ids = lax.broadcasted_iota(jnp.int32, (P_KEYS, tm), 0).astype(_F32)
    cand_ids = cid_ref[...]
    experts, gates = [], []
    for h in range(P_HEADS):
        tops = []
        for c in range(2):
            qc = qry_ref[:, (2 * h + c) * P_HALF:(2 * h + c + 1) * P_HALF]
            s = lax.dot_general(keys_ref[2 * h + c], qc, _NT, preferred_element_type=_F32)
            tops.append(_top_rows(s, key_ids, P_TOPK))
        (v1, i1), (v2, i2) = tops
        s2 = _stack_rows(v2)
        s2_low = s2[0:SUBLANES]
        blocks = [v1[0] + s2]
        blocks += [v1[a] + s2_low for a in range(1, SUBLANES)]
        blocks.append(_stack_rows(v1[SUBLANES:]) + v2[0])
        cand = jnp.concatenate(blocks, axis=0)
        best, pos = _top_rows(cand, cand_ids, P_TOPK)
        best = _stack_rows(best)
        pos = _stack_rows(pos).astype(jnp.int32)
        pa = lax.shift_right_logical(pos, 4)
        pb = jnp.bitwise_and(pos, P_TOPK - 1)
        e1 = jnp.zeros_like(best)
        e2 = jnp.zeros_like(best)
        for a in range(P_TOPK):
            e1 = jnp.where(pa == a, i1[a], e1)
            e2 = jnp.where(pb == a, i2[a], e2)
        experts.append((e1 * P_KEYS + e2).astype(jnp.int32))
        ex = jnp.exp(best - best[0:1])
        gates.append(ex / jnp.sum(ex, axis=0, keepdims=True))
    half = P_HEADS // 2
    lo = jnp.concatenate(experts[:half], axis=0) * EXPERT_ROWS
    hi = jnp.concatenate(experts[half:], axis=0) * EXPERT_ROWS
    idx_ref[...] = (lo | lax.shift_left(hi, 16)).T
    gate_ref[...] = jnp.concatenate(gates, axis=0).T


def _peer_route(qry, keys_b, tm):
    n, w = qry.shape
    row = lambda i: (i, 0)
    cid = jnp.asarray(np.broadcast_to(_candidate_ids()[:, None], (80, tm)))
    return pl.pallas_call(
        _peer_route_kernel,
        grid=(n // tm,),
        in_specs=[pl.BlockSpec((tm, w), row), _const_spec(keys_b.shape), _const_spec(cid.shape)],
        out_specs=[pl.BlockSpec((tm, P_SLOTS // 2), row), pl.BlockSpec((tm, P_SLOTS), row)],
        out_shape=[jax.ShapeDtypeStruct((n, P_SLOTS // 2), jnp.int32),
                   jax.ShapeDtypeStruct((n, P_SLOTS), _F32)],
        compiler_params=_params(1),
        name="peer_route",
    )(qry, keys_b, cid)


def _pack_rows(x):
    n, d = x.shape
    bits = lax.bitcast_convert_type(x.astype(_BF16), jnp.uint16).astype(jnp.uint32)
    bits = bits.reshape(n, EXPERT_ROWS, 2, LANES)
    packed = bits[:, :, 0, :] | (bits[:, :, 1, :] << 16)
    return lax.bitcast_convert_type(packed, jnp.int32).reshape(n * EXPERT_ROWS, LANES)


def _slot_masks():
    c = np.arange(P_SLOTS * SUBLANES)
    sel = (c[None, :] % SUBLANES == np.arange(GROUP_ROWS)[:, None]).astype(np.float32)
    spread = (c[None, :] // SUBLANES == np.arange(P_SLOTS)[:, None]).astype(np.float32)
    return jnp.asarray(sel), jnp.asarray(spread, _BF16)


def _load_table_once(tbl_hbm, tbl_ref, sem):
    @pl.when(pl.program_id(0) == 0)
    def _():
        cp = pltpu.make_async_copy(tbl_hbm, tbl_ref, sem)
        cp.start()
        cp.wait()


def _stage_rows(idx_ref, tbl_ref, stage_ref, base):
    half = P_SLOTS // 2
    for tt in range(TOKEN_GROUP):
        for j in range(half):
            word = idx_ref[base + tt, j]
            offs = (jnp.bitwise_and(word, 0xFFFF), lax.shift_right_logical(word, 16))
            for slot, off in zip((j, j + half), offs):
                stage_ref[tt, slot * EXPERT_ROWS:(slot + 1) * EXPERT_ROWS, :] = tbl_ref[
                    pl.ds(pl.multiple_of(off, EXPERT_ROWS), EXPERT_ROWS), :]


def _gelu(a):
    return 0.5 * a * (1.0 + lax.erf(a * (2.0 ** -0.5)))


def _peer_up_kernel(idx_ref, h_ref, gate_ref, sel_ref, fold_ref, tbl_hbm, act_ref,
                    tbl_ref, stage_ref, a_ref, sem):
    _load_table_once(tbl_hbm, tbl_ref, sem)
    tm = h_ref.shape[0]

    def group(g, carry):
        base = pl.multiple_of(g * TOKEN_GROUP, TOKEN_GROUP)
        _stage_rows(idx_ref, tbl_ref, stage_ref, base)
        parts = []
        for tt in range(TOKEN_GROUP):
            rows = pltpu.bitcast(stage_ref[tt], _BF16)
            hq = pltpu.bitcast(h_ref[base + tt], _BF16)
            r = lax.dot_general(hq, rows, _NT, preferred_element_type=_F32)
            parts.append(r * sel_ref[...])
        r = jnp.concatenate(parts, axis=0)
        hi = r.astype(_BF16)
        lo = (r - hi.astype(_F32)).astype(_BF16)
        a = (jnp.dot(hi, fold_ref[...], preferred_element_type=_F32)
             + jnp.dot(lo, fold_ref[...], preferred_element_type=_F32))
        a_ref[pl.ds(base, TOKEN_GROUP), :] = jnp.sum(
            a.reshape(TOKEN_GROUP, GROUP_ROWS, P_SLOTS), axis=1)
        return carry

    lax.fori_loop(0, tm // TOKEN_GROUP, group, 0)
    act_ref[...] = _gelu(a_ref[...]) * gate_ref[...]


def _peer_up(idx, hp, gate, table, tm):
    n = idx.shape[0]
    sel, spread = _slot_masks()
    fold = spread.T
    row = lambda i: (i, 0)
    return pl.pallas_call(
        _peer_up_kernel,
        grid=(n // tm,),
        in_specs=[pl.BlockSpec((tm, P_SLOTS // 2), row, memory_space=pltpu.SMEM),
                  pl.BlockSpec((tm, SUBLANES, LANES), lambda i: (i, 0, 0)),
                  pl.BlockSpec((tm, P_SLOTS), row), _const_spec(sel.shape),
                  _const_spec(fold.shape), pl.BlockSpec(memory_space=pl.ANY)],
        out_specs=pl.BlockSpec((tm, P_SLOTS), row),
        out_shape=jax.ShapeDtypeStruct((n, P_SLOTS), _F32),
        scratch_shapes=[pltpu.VMEM(table.shape, jnp.int32),
                        pltpu.VMEM((TOKEN_GROUP, P_SLOTS * EXPERT_ROWS, LANES), jnp.int32),
                        pltpu.VMEM((tm, P_SLOTS), _F32),
                        pltpu.SemaphoreType.DMA(())],
        compiler_params=_params(1),
        name="peer_up",
    )(idx, hp, gate, sel, fold, table)


def _peer_down_kernel(idx_ref, act_ref, x_ref, sel_ref, spread_ref, tbl_hbm, y_ref,
                      tbl_ref, stage_ref, sem):
    _load_table_once(tbl_hbm, tbl_ref, sem)
    tm = x_ref.shape[0]

    def group(g, carry):
        base = pl.multiple_of(g * TOKEN_GROUP, TOKEN_GROUP)
        _stage_rows(idx_ref, tbl_ref, stage_ref, base)
        act = act_ref[pl.ds(base, TOKEN_GROUP), :]
        rep = jnp.concatenate(
            [jnp.broadcast_to(act[tt:tt + 1], (GROUP_ROWS, P_SLOTS)) for tt in range(TOKEN_GROUP)],
            axis=0).astype(_BF16)
        w = jnp.dot(rep, spread_ref[...], preferred_element_type=_F32)
        for tt in range(TOKEN_GROUP):
            wt = (w[tt * GROUP_ROWS:(tt + 1) * GROUP_ROWS] * sel_ref[...]).astype(_BF16)
            rows = pltpu.bitcast(stage_ref[tt], _BF16)
            out = jnp.dot(wt, rows, preferred_element_type=_F32)
            y_ref[base + tt] = x_ref[base + tt] + out[0:SUBLANES]
        return carry

    lax.fori_loop(0, tm // TOKEN_GROUP, group, 0)


def _peer_down(idx, act, x3, table, tm):
    n = idx.shape[0]
    sel, spread = _slot_masks()
    row = lambda i: (i, 0)
    tok = pl.BlockSpec((tm, SUBLANES, LANES), lambda i: (i, 0, 0))
    return pl.pallas_call(
        _peer_down_kernel,
        grid=(n // tm,),
        in_specs=[pl.BlockSpec((tm, P_SLOTS // 2), row, memory_space=pltpu.SMEM),
                  pl.BlockSpec((tm, P_SLOTS), row), tok, _const_spec(sel.shape),
                  _const_spec(spread.shape), pl.BlockSpec(memory_space=pl.ANY)],
        out_specs=tok,
        out_shape=jax.ShapeDtypeStruct(x3.shape, _F32),
        scratch_shapes=[pltpu.VMEM(table.shape, jnp.int32),
                        pltpu.VMEM((TOKEN_GROUP, P_SLOTS * EXPERT_ROWS, LANES), jnp.int32),
                        pltpu.SemaphoreType.DMA(())],
        compiler_params=_params(1),
        name="peer_down",
    )(idx, act, x3, sel, spread, table)


def _rope_tables(pos):
    half = R_KEY_DIM // 2
    inv = ROPE_BASE ** (-jnp.arange(half, dtype=_F32) / half)
    ang = pos.astype(_F32)[:, None] * inv[None, :]
    cos, sin = jnp.cos(ang), jnp.sin(ang)
    return jnp.concatenate([cos, cos], axis=1), jnp.concatenate([-sin, sin], axis=1)


def _pick_tile(n, want):
    while n % want:
        want //= 2
    return want


def _layer(x, pos, weights, attend, s0, ret_len):
    (mix_g, w_in_b, hsum, qg, kg, r_gain, wa, wr, wo, fg, wq, keys_b, tbl_u, tbl_v) = weights
    b, t, d = x.shape
    n = b * t
    x2 = x.reshape(n, d)
    tm = _pick_tile(n, 512)
    cs, sn = _rope_tables(jnp.tile(pos, max(tm // t, 1)))
    qa, ka, va, qr, kr, vr, gr, ga, gt = _in_proj(x2, mix_g, w_in_b, hsum, qg, kg, cs, sn, tm)
    shp = lambda a: a.reshape(b, t, a.shape[1])
    oa = attend(shp(qa), shp(ka), shp(va))
    og, s_new = _retention(shp(qr), shp(kr), shp(vr), shp(gr), s0, r_gain, ret_len)
    x1, h2, qry = _merge(x2, oa.reshape(n, -1), og.reshape(n, -1), ga, gt, wa, wr, wo, fg, wq, tm)
    tr = _pick_tile(n, 256)
    idx, gate = _peer_route(qry, keys_b, tr)
    tp = _pick_tile(n, 128)
    hw = _pack_rows(h2).reshape(n, EXPERT_ROWS, LANES)
    act = _peer_up(idx, jnp.concatenate([hw, hw], axis=1), gate, tbl_u, tp)
    y = _peer_down(idx, act, x1.reshape(n, SUBLANES, LANES), tbl_v, tp)
    return y.reshape(b, t, d), shp(ka), shp(va), s_new


def kernel(x_prompt, x_sample, cache_attn_k, cache_attn_v, state_retention, mix_norm_g, w_in, attn_q_norm_g, attn_k_norm_g, attn_rel_bias, ret_out_norm_g, w_branch_attn, w_branch_ret, w_out, ffn_norm_g, peer_w_query, peer_sub_keys, peer_expert_u, peer_expert_v):
    depth = w_in.shape[0]
    assert depth == 1, "single-layer step"
    bp, s, d = x_prompt.shape
    bs, t, _ = x_sample.shape
    l = 0
    hsum = jnp.asarray(np.kron(np.eye(A_HEADS), np.ones((A_HEAD_DIM, A_HEAD_DIM))), _BF16)
    weights = (
        mix_norm_g[l][None, :], w_in[l].astype(_BF16), hsum,
        jnp.tile(attn_q_norm_g[l], A_HEADS)[None, :], jnp.tile(attn_k_norm_g[l], A_HEADS)[None, :],
        ret_out_norm_g[l].reshape(1, R_WIDTH),
        w_branch_attn[l].astype(_BF16), w_branch_ret[l].astype(_BF16), w_out[l].astype(_BF16),
        ffn_norm_g[l][None, :], peer_w_query[l].astype(_BF16),
        peer_sub_keys[l].reshape(2 * P_HEADS, P_KEYS, P_HALF).astype(_BF16),
        _pack_rows(peer_expert_u[l]), _pack_rows(peer_expert_v[l]),
    )
    rel_bias = attn_rel_bias[l]

    tq = 256
    n_past = BAND_PAST // tq
    rel_p = _rel_bias_row(rel_bias, tq, BAND_PAST + tq)

    def attend_prompt(q, k, v):
        def slots(a):
            return [(a, tq, functools.partial(
                lambda bi, i, back: (bi, jnp.maximum(i - back, 0), 0), back=n_past - sl))
                for sl in range(n_past + 1)]
        return _band_attn(q, slots(k), slots(v), rel_p, tq, BAND_PAST + tq, mask_leading=True)

    rows = cache_attn_k.shape[2]
    ck = cache_attn_k[l].reshape(bs, rows, A_WIDTH)
    cv = cache_attn_v[l].reshape(bs, rows, A_WIDTH)
    rel_s = _rel_bias_row(rel_bias, t, rows + t)

    def attend_sample(q, k, v):
        zero = lambda bi, i: (bi, 0, 0)
        return _band_attn(q, [(ck, rows, zero), (k, t, zero)], [(cv, rows, zero), (v, t, zero)],
                          rel_s, t, rows + t, mask_leading=False)

    s0_p = jnp.zeros((bp, R_HEADS, R_KEY_DIM, R_KEY_DIM), _F32)
    y_p, k_p, v_p, s_p = _layer(x_prompt, jnp.arange(s), weights, attend_prompt, s0_p, 256)
    y_s, k_s, v_s, s_s = _layer(x_sample, PAST_LEN + jnp.arange(t), weights, attend_sample,
                                state_retention[l], t)
    keep = min(BAND_PAST, s)
    heads = lambda a: a.reshape(a.shape[0], a.shape[1], A_HEADS, A_HEAD_DIM)[None]
    return (y_p, y_s, heads(k_p[:, s - keep:]), heads(v_p[:, s - keep:]), s_p[None],
            heads(k_s), heads(v_s), s_s[None])
```

```python
import functools

import numpy as np
import jax
import jax.numpy as jnp
from jax import lax
from jax.experimental import pallas as pl
from jax.experimental.pallas import tpu as pltpu

PAST_LEN = 1024
CHUNK = 64
EPS = 1e-6
A_HEADS = 8
A_HEAD_DIM = 64
A_WIDTH = A_HEADS * A_HEAD_DIM
BAND_CHUNKS = 8
BAND_PAST = BAND_CHUNKS * CHUNK
REL_CLIP = 128
MASK_VALUE = -1e30
R_HEADS = 4
R_KEY_DIM = 128
R_WIDTH = R_HEADS * R_KEY_DIM
ROPE_BASE = 10000.0
P_HEADS = 8
P_KEYS = 128
P_HALF = 128
P_TOPK = 16
P_SLOTS = P_HEADS * P_TOPK

LANES = 128
SUBLANES = 8
VMEM_LIMIT = 56 * 1024 * 1024

EXPERT_ROWS = 4
TOKEN_GROUP = 16
GROUP_ROWS = 2 * SUBLANES

_NT = (((1,), (1,)), ((), ()))
_F32 = jnp.float32
_BF16 = jnp.bfloat16


def _params(n_axes, vmem=VMEM_LIMIT):
    return pltpu.CompilerParams(
        dimension_semantics=("arbitrary",) * n_axes, vmem_limit_bytes=vmem)


def _const_spec(shape):
    nd = len(shape)
    return pl.BlockSpec(shape, lambda *_: (0,) * nd)


def _rotate_heads(x, cs, sn):
    outs = []
    for h in range(R_HEADS):
        xh = x[:, h * R_KEY_DIM:(h + 1) * R_KEY_DIM]
        outs.append(xh * cs + pltpu.roll(xh, R_KEY_DIM // 2, axis=1) * sn)
    return jnp.concatenate(outs, axis=1)


def _in_proj_kernel(x_ref, g_ref, w_ref, hsum_ref, qg_ref, kg_ref, cs_ref, sn_ref,
                    qa_ref, ka_ref, va_ref, qr_ref, kr_ref, vr_ref, gr_ref, ga_ref, gt_ref):
    x = x_ref[...]
    h = x * lax.rsqrt(jnp.mean(x * x, axis=-1, keepdims=True) + EPS) * g_ref[...]
    hb = h.astype(_BF16)

    def seg(start, width):
        return jnp.dot(hb, w_ref[:, start:start + width], preferred_element_type=_F32)

    def head_norm(z, gain):
        ss = jnp.dot((z * z).astype(_BF16), hsum_ref[...], preferred_element_type=_F32)
        return z * lax.rsqrt(ss * (1.0 / A_HEAD_DIM) + EPS) * gain

    qa_ref[...] = head_norm(seg(0, A_WIDTH), qg_ref[...]).astype(_BF16)
    ka_ref[...] = head_norm(seg(A_WIDTH, A_WIDTH), kg_ref[...])
    va_ref[...] = seg(2 * A_WIDTH, A_WIDTH)
    o = 3 * A_WIDTH
    cs = cs_ref[...]
    sn = sn_ref[...]
    qr_ref[...] = _rotate_heads(seg(o, R_WIDTH), cs, sn).astype(_BF16)
    kr_ref[...] = (_rotate_heads(seg(o + R_WIDTH, R_WIDTH), cs, sn)
                   * (R_KEY_DIM ** -0.5)).astype(_BF16)
    vr_ref[...] = seg(o + 2 * R_WIDTH, R_WIDTH).astype(_BF16)
    gr_ref[...] = seg(o + 3 * R_WIDTH, R_WIDTH)
    o += 4 * R_WIDTH
    d = x.shape[-1]
    ga_ref[...] = seg(o, d)
    gt_ref[...] = seg(o + d, d)


def _in_proj(x2, mix_g, w_in_b, hsum, qg, kg, cs, sn, tm):
    n, d = x2.shape
    period = cs.shape[0] // tm
    row = lambda i: (i, 0)
    pos = lambda i: (i % period, 0)
    wide = lambda w, dt: jax.ShapeDtypeStruct((n, w), dt)
    outs = [wide(A_WIDTH, _BF16), wide(A_WIDTH, _F32), wide(A_WIDTH, _F32),
            wide(R_WIDTH, _BF16), wide(R_WIDTH, _BF16), wide(R_WIDTH, _BF16),
            wide(R_WIDTH, _F32), wide(d, _F32), wide(d, _F32)]
    return pl.pallas_call(
        _in_proj_kernel,
        grid=(n // tm,),
        in_specs=[pl.BlockSpec((tm, d), row), _const_spec(mix_g.shape),
                  _const_spec(w_in_b.shape), _const_spec(hsum.shape),
                  _const_spec(qg.shape), _const_spec(kg.shape),
                  pl.BlockSpec((tm, LANES), pos), pl.BlockSpec((tm, LANES), pos)],
        out_specs=[pl.BlockSpec((tm, o.shape[1]), row) for o in outs],
        out_shape=outs,
        compiler_params=_params(1),
        name="in_proj",
    )(x2, mix_g, w_in_b, hsum, qg, kg, cs, sn)


def _band_attn_kernel(*refs, n_slots, mask_leading):
    q_ref = refs[0]
    k_refs = refs[1:1 + n_slots]
    v_refs = refs[1 + n_slots:1 + 2 * n_slots]
    rel_ref = refs[1 + 2 * n_slots]
    o_ref = refs[2 + 2 * n_slots]
    bias_ref = refs[3 + 2 * n_slots]
    i = pl.program_id(1)
    tq = q_ref.shape[1]
    tk = bias_ref.shape[2]

    @pl.when((pl.program_id(0) == 0) & (i == 0))
    def _():
        qc = lax.shift_right_logical(lax.broadcasted_iota(jnp.int32, (tq, tk), 0), 6)
        kc = lax.shift_right_logical(lax.broadcasted_iota(jnp.int32, (tq, tk), 1), 6)
        valid = (kc >= qc) & (kc <= qc + BAND_CHUNKS)
        for h in range(A_HEADS):
            row = jnp.broadcast_to(rel_ref[h:h + 1, :], (tq, rel_ref.shape[1]))
            toep = pltpu.roll(row, tk, 1, stride=1, stride_axis=0)
            bias_ref[h] = jnp.where(valid, toep[:, :tk], MASK_VALUE)

    low = lax.broadcasted_iota(jnp.int32, (1, LANES), 1) < A_HEAD_DIM
    offs = []
    for s in range(n_slots):
        if mask_leading and s < n_slots - 1:
            offs.append(jnp.where(i >= n_slots - 1 - s, 0.0, MASK_VALUE).astype(_F32))
        else:
            offs.append(None)
    for hp in range(A_HEADS // 2):
        cols = slice(hp * LANES, (hp + 1) * LANES)
        q2 = q_ref[0, :, cols] * (A_HEAD_DIM ** -0.5)
        zero = jnp.zeros_like(q2)
        ks = [k_refs[s][0, :, cols].astype(_BF16) for s in range(n_slots)]
        vs = [v_refs[s][0, :, cols].astype(_BF16) for s in range(n_slots)]
        halves = []
        for half in range(2):
            qh = jnp.where(low, q2, zero) if half == 0 else jnp.where(low, zero, q2)
            parts = []
            for s in range(n_slots):
                sc = lax.dot_general(qh, ks[s], _NT, preferred_element_type=_F32)
                if offs[s] is not None:
                    sc = sc + offs[s]
                parts.append(sc)
            sc = jnp.concatenate(parts, axis=1) + bias_ref[2 * hp + half]
            m = jnp.max(sc, axis=-1, keepdims=True)
            p = jnp.exp(sc - m)
            l = jnp.sum(p, axis=-1, keepdims=True)
            pb = p.astype(_BF16)
            acc = jnp.zeros((tq, LANES), _F32)
            c0 = 0
            for s in range(n_slots):
                w = ks[s].shape[0]
                acc = acc + jnp.dot(pb[:, c0:c0 + w], vs[s], preferred_element_type=_F32)
                c0 += w
            halves.append(acc / l)
        o_ref[0, :, cols] = jnp.where(low, halves[0], halves[1]).astype(_BF16)


def _band_attn(q, k_slots, v_slots, rel_rev, tq, tk, mask_leading):
    b, t, w = q.shape
    n_slots = len(k_slots)
    specs = [pl.BlockSpec((1, tq, w), lambda bi, i: (bi, i, 0))]
    args = [q]
    for arr, rows, imap in list(k_slots) + list(v_slots):
        specs.append(pl.BlockSpec((1, rows, w), imap))
        args.append(arr)
    specs.append(_const_spec(rel_rev.shape))
    args.append(rel_rev)
    return pl.pallas_call(
        functools.partial(_band_attn_kernel, n_slots=n_slots, mask_leading=mask_leading),
        grid=(b, t // tq),
        in_specs=specs,
        out_specs=pl.BlockSpec((1, tq, w), lambda bi, i: (bi, i, 0)),
        out_shape=jax.ShapeDtypeStruct((b, t, w), _BF16),
        scratch_shapes=[pltpu.VMEM((A_HEADS, tq, tk), _F32)],
        compiler_params=_params(2),
        name="band_attn",
    )(*args)


def _rel_bias_row(rel_bias, tq, tk):
    n = -(-(tq + tk) // LANES) * LANES
    first = 1 + REL_CLIP - tq
    left = max(-first, 0)
    body = rel_bias.astype(_F32)[:, max(first, 0):]
    row = jnp.pad(body, ((0, 0), (left, n - left - body.shape[1])), mode="edge")
    return row[:, ::-1]


def _retention_kernel(q_ref, k_ref, v_ref, gr_ref, s0_ref, dec_ref, qd_ref, kd_ref, sd_ref,
                      rg_ref, o_ref, s_ref):
    @pl.when(pl.program_id(1) == 0)
    def _():
        s_ref[...] = s0_ref[...]

    for h in range(R_HEADS):
        cols = slice(h * R_KEY_DIM, (h + 1) * R_KEY_DIM)
        q = q_ref[0, :, cols]
        k = k_ref[0, :, cols]
        v = v_ref[0, :, cols]
        state = s_ref[0, h]
        inner = lax.dot_general(q, k, _NT, preferred_element_type=_F32) * dec_ref[h]
        o = jnp.dot(inner.astype(_BF16), v, preferred_element_type=_F32)
        o = o + jnp.dot(q, state.astype(_BF16), preferred_element_type=_F32) * qd_ref[:, cols]
        kt = (k.astype(_F32) * kd_ref[:, cols]).T.astype(_BF16)
        s_ref[0, h] = state * sd_ref[:, cols] + jnp.dot(kt, v, preferred_element_type=_F32)
        on = o * lax.rsqrt(jnp.mean(o * o, axis=-1, keepdims=True) + EPS) * rg_ref[:, cols]
        g = gr_ref[0, :, cols]
        o_ref[0, :, cols] = (on * (g * jax.nn.sigmoid(g))).astype(_BF16)


def _retention_tables(length):
    log_gamma = jnp.log1p(-(2.0 ** (-5.0 - jnp.arange(R_HEADS, dtype=_F32))))
    idx = jnp.arange(length, dtype=_F32)
    diff = idx[:, None] - idx[None, :]
    decay = jnp.where(diff >= 0, jnp.exp(log_gamma[:, None, None] * jnp.maximum(diff, 0.0)), 0.0)
    wide = lambda a: jnp.repeat(a, R_KEY_DIM, axis=-1)
    q_decay = wide(jnp.exp(log_gamma[None, :] * (idx[:, None] + 1.0)))
    k_decay = wide(jnp.exp(log_gamma[None, :] * (length - 1.0 - idx)[:, None]))
    s_decay = wide(jnp.exp(log_gamma * length)[None, :])
    return decay, q_decay, k_decay, s_decay


def _retention(qr, kr, vr, gr, s0, r_gain, length):
    b, t, w = qr.shape
    decay, q_decay, k_decay, s_decay = _retention_tables(length)
    blk = pl.BlockSpec((1, length, w), lambda bi, i: (bi, i, 0))
    st = pl.BlockSpec((1,) + s0.shape[1:], lambda bi, i: (bi, 0, 0, 0))
    return pl.pallas_call(
        _retention_kernel,
        grid=(b, t // length),
        in_specs=[blk, blk, blk, blk, st, _const_spec(decay.shape), _const_spec(q_decay.shape),
                  _const_spec(k_decay.shape), _const_spec(s_decay.shape),
                  _const_spec(r_gain.shape)],
        out_specs=[blk, st],
        out_shape=[jax.ShapeDtypeStruct((b, t, w), _BF16),
                   jax.ShapeDtypeStruct(s0.shape, _F32)],
        compiler_params=_params(2),
        name="retention",
    )(qr, kr, vr, gr, s0, decay, q_decay, k_decay, s_decay, r_gain)


def _merge_kernel(x_ref, oa_ref, og_ref, ga_ref, gt_ref, wa_ref, wr_ref, wo_ref, fg_ref, wq_ref,
                  x1_ref, h2_ref, qry_ref):
    br_a = jnp.dot(oa_ref[...], wa_ref[...], preferred_element_type=_F32)
    br_r = jnp.dot(og_ref[...], wr_ref[...], preferred_element_type=_F32)
    merged = jax.nn.sigmoid(ga_ref[...]) * br_a + jax.nn.sigmoid(gt_ref[...]) * br_r
    x1 = x_ref[...] + jnp.dot(merged.astype(_BF16), wo_ref[...], preferred_element_type=_F32)
    x1_ref[...] = x1
    h2 = (x1 * lax.rsqrt(jnp.mean(x1 * x1, axis=-1, keepdims=True) + EPS)
          * fg_ref[...]).astype(_BF16)
    h2_ref[...] = h2
    qry_ref[...] = jnp.dot(h2, wq_ref[...], preferred_element_type=_F32).astype(_BF16)


def _merge(x2, oa, og, ga, gt, wa, wr, wo, fg, wq, tm):
    n, d = x2.shape
    row = lambda i: (i, 0)
    blk = lambda a: pl.BlockSpec((tm, a.shape[1]), row)
    outs = [jax.ShapeDtypeStruct((n, d), _F32), jax.ShapeDtypeStruct((n, d), _BF16),
            jax.ShapeDtypeStruct((n, wq.shape[1]), _BF16)]
    return pl.pallas_call(
        _merge_kernel,
        grid=(n // tm,),
        in_specs=[blk(x2), blk(oa), blk(og), blk(ga), blk(gt), _const_spec(wa.shape),
                  _const_spec(wr.shape), _const_spec(wo.shape), _const_spec(fg.shape),
                  _const_spec(wq.shape)],
        out_specs=[pl.BlockSpec((tm, o.shape[1]), row) for o in outs],
        out_shape=outs,
        compiler_params=_params(1),
        name="merge",
    )(x2, oa, og, ga, gt, wa, wr, wo, fg, wq)


_NO_ROW = 1e9


def _top_rows(s, ids, k):
    vals, picked = [], []
    for _ in range(k):
        m = jnp.max(s, axis=0, keepdims=True)
        am = jnp.min(jnp.where(s == m, ids, _NO_ROW), axis=0, keepdims=True)
        s = jnp.where(ids == am, -jnp.inf, s)
        vals.append(m)
        picked.append(am)
    return vals, picked


def _stack_rows(rows_list):
    k = len(rows_list)
    rid = lax.broadcasted_iota(jnp.int32, (k, rows_list[0].shape[1]), 0)
    out = jnp.broadcast_to(rows_list[0], rid.shape)
    for r in range(1, k):
        out = jnp.where(rid == r, rows_list[r], out)
    return out


def _candidate_ids():
    ids = [b for b in range(P_TOPK)]
    for a in range(1, SUBLANES):
        ids += [a * P_TOPK + b for b in range(SUBLANES)]
    ids += [a * P_TOPK for a in range(SUBLANES, P_TOPK)]
    return np.asarray(ids, np.float32)


def _peer_route_kernel(qry_ref, keys_ref, cid_ref, idx_ref, gate_ref):
    tm = qry_ref.shape[0]
    key_ids = lax.broadcasted_iota(jnp.int32, (P_KEYS, tm), 0).astype(_F32)
    cand_ids = cid_ref[...]
    experts, gates = [], []
    for h in range(P_HEADS):
        tops = []
        for c in range(2):
            qc = qry_ref[:, (2 * h + c) * P_HALF:(2 * h + c + 1) * P_HALF]
            s = lax.dot_general(keys_ref[2 * h + c], qc, _NT, preferred_element_type=_F32)
            tops.append(_top_rows(s, key_ids, P_TOPK))
        (v1, i1), (v2, i2) = tops
        s2 = _stack_rows(v2)
        s2_low = s2[0:SUBLANES]
        blocks = [v1[0] + s2]
        blocks += [v1[a] + s2_low for a in range(1, SUBLANES)]
        blocks.append(_stack_rows(v1[SUBLANES:]) + v2[0])
        cand = jnp.concatenate(blocks, axis=0)
        best, pos = _top_rows(cand, cand_ids, P_TOPK)
        best = _stack_rows(best)
        pos = _stack_rows(pos).astype(jnp.int32)
        pa = lax.shift_right_logical(pos, 4)
        pb = jnp.bitwise_and(pos, P_TOPK - 1)
        e1 = jnp.zeros_like(best)
        e2 = jnp.zeros_like(best)
        for a in range(P_TOPK):
            e1 = jnp.where(pa == a, i1[a], e1)
            e2 = jnp.where(pb == a, i2[a], e2)
        experts.append((e1 * P_KEYS + e2).astype(jnp.int32))
        ex = jnp.exp(best - best[0:1])
        gates.append(ex / jnp.sum(ex, axis=0, keepdims=True))
    half = P_HEADS // 2
    lo = jnp.concatenate(experts[:half], axis=0) * EXPERT_ROWS
    hi = jnp.concatenate(experts[half:], axis=0) * EXPERT_ROWS
    idx_ref[...] = (lo | lax.shift_left(hi, 16)).T
    gate_ref[...] = jnp.concatenate(gates, axis=0).T


def _peer_route(qry, keys_b, tm):
    n, w = qry.shape
    row = lambda i: (i, 0)
    cid = jnp.asarray(np.broadcast_to(_candidate_ids()[:, None], (80, tm)))
    return pl.pallas_call(
        _peer_route_kernel,
        grid=(n // tm,),
        in_specs=[pl.BlockSpec((tm, w), row), _const_spec(keys_b.shape), _const_spec(cid.shape)],
        out_specs=[pl.BlockSpec((tm, P_SLOTS // 2), row), pl.BlockSpec((tm, P_SLOTS), row)],
        out_shape=[jax.ShapeDtypeStruct((n, P_SLOTS // 2), jnp.int32),
                   jax.ShapeDtypeStruct((n, P_SLOTS), _F32)],
        compiler_params=_params(1),
        name="peer_route",
    )(qry, keys_b, cid)


def _pack_rows(x):
    n, d = x.shape
    bits = lax.bitcast_convert_type(x.astype(_BF16), jnp.uint16).astype(jnp.uint32)
    bits = bits.reshape(n, EXPERT_ROWS, 2, LANES)
    packed = bits[:, :, 0, :] | (bits[:, :, 1, :] << 16)
    return lax.bitcast_convert_type(packed, jnp.int32).reshape(n * EXPERT_ROWS, LANES)


def _slot_masks():
    c = np.arange(P_SLOTS * SUBLANES)
    sel = (c[None, :] % SUBLANES == np.arange(GROUP_ROWS)[:, None]).astype(np.float32)
    spread = (c[None, :] // SUBLANES == np.arange(P_SLOTS)[:, None]).astype(np.float32)
    return jnp.asarray(sel), jnp.asarray(spread, _BF16)


def _table_spec(table):
    return pl.BlockSpec(table.shape, lambda i: (0, 0), pipeline_mode=pl.Buffered(1))


def _token_rows(idx_ref, tbl_ref, t):
    half = P_SLOTS // 2
    lo, hi = [], []
    for j in range(half):
        word = idx_ref[t, j]
        offs = (jnp.bitwise_and(word, 0xFFFF), lax.shift_right_logical(word, 16))
        for dst, off in zip((lo, hi), offs):
            dst.append(tbl_ref[pl.ds(pl.multiple_of(off, EXPERT_ROWS), EXPERT_ROWS), :])
    return jnp.concatenate(lo + hi, axis=0)


def _gelu(a):
    return 0.5 * a * (1.0 + lax.erf(a * (2.0 ** -0.5)))


def _peer_up_kernel(idx_ref, h_ref, gate_ref, sel_ref, fold_ref, tbl_ref, act_ref, a_ref):
    tm = h_ref.shape[0]

    def group(g, carry):
        base = pl.multiple_of(g * TOKEN_GROUP, TOKEN_GROUP)
        parts = []
        for tt in range(TOKEN_GROUP):
            rows = pltpu.bitcast(_token_rows(idx_ref, tbl_ref, base + tt), _BF16)
            hq = pltpu.bitcast(h_ref[base + tt], _BF16)
            r = lax.dot_general(hq, rows, _NT, preferred_element_type=_F32)
            parts.append(r * sel_ref[...])
        r = jnp.concatenate(parts, axis=0)
        hi = r.astype(_BF16)
        lo = (r - hi.astype(_F32)).astype(_BF16)
        a = (jnp.dot(hi, fold_ref[...], preferred_element_type=_F32)
             + jnp.dot(lo, fold_ref[...], preferred_element_type=_F32))
        a_ref[pl.ds(base, TOKEN_GROUP), :] = jnp.sum(
            a.reshape(TOKEN_GROUP, GROUP_ROWS, P_SLOTS), axis=1)
        return carry

    lax.fori_loop(0, tm // TOKEN_GROUP, group, 0)
    act_ref[...] = _gelu(a_ref[...]) * gate_ref[...]


def _peer_up(idx, hp, gate, table, tm):
    n = idx.shape[0]
    sel, spread = _slot_masks()
    fold = spread.T
    row = lambda i: (i, 0)
    return pl.pallas_call(
        _peer_up_kernel,
        grid=(n // tm,),
        in_specs=[pl.BlockSpec((tm, P_SLOTS // 2), row, memory_space=pltpu.SMEM),
                  pl.BlockSpec((tm, SUBLANES, LANES), lambda i: (i, 0, 0)),
                  pl.BlockSpec((tm, P_SLOTS), row), _const_spec(sel.shape),
                  _const_spec(fold.shape), _table_spec(table)],
        out_specs=pl.BlockSpec((tm, P_SLOTS), row),
        out_shape=jax.ShapeDtypeStruct((n, P_SLOTS), _F32),
        scratch_shapes=[pltpu.VMEM((tm, P_SLOTS), _F32)],
        compiler_params=_params(1),
        name="peer_up",
    )(idx, hp, gate, sel, fold, table)


def _peer_down_kernel(idx_ref, act_ref, x_ref, sel_ref, spread_ref, tbl_ref, y_ref):
    tm = x_ref.shape[0]

    def group(g, carry):
        base = pl.multiple_of(g * TOKEN_GROUP, TOKEN_GROUP)
        act = act_ref[pl.ds(base, TOKEN_GROUP), :]
        rep = jnp.concatenate(
            [jnp.broadcast_to(act[tt:tt + 1], (GROUP_ROWS, P_SLOTS)) for tt in range(TOKEN_GROUP)],
            axis=0).astype(_BF16)
        w = jnp.dot(rep, spread_ref[...], preferred_element_type=_F32)
        for tt in range(TOKEN_GROUP):
            wt = (w[tt * GROUP_ROWS:(tt + 1) * GROUP_ROWS] * sel_ref[...]).astype(_BF16)
            rows = pltpu.bitcast(_token_rows(idx_ref, tbl_ref, base + tt), _BF16)
            out = jnp.dot(wt, rows, preferred_element_type=_F32)
            y_ref[base + tt] = x_ref[base + tt] + out[0:SUBLANES]
        return carry

    lax.fori_loop(0, tm // TOKEN_GROUP, group, 0)


def _peer_down(idx, act, x3, table, tm):
    n = idx.shape[0]
    sel, spread = _slot_masks()
    row = lambda i: (i, 0)
    tok = pl.BlockSpec((tm, SUBLANES, LANES), lambda i: (i, 0, 0))
    return pl.pallas_call(
        _peer_down_kernel,
        grid=(n // tm,),
        in_specs=[pl.BlockSpec((tm, P_SLOTS // 2), row, memory_space=pltpu.SMEM),
                  pl.BlockSpec((tm, P_SLOTS), row), tok, _const_spec(sel.shape),
                  _const_spec(spread.shape), _table_spec(table)],
        out_specs=tok,
        out_shape=jax.ShapeDtypeStruct(x3.shape, _F32),
        compiler_params=_params(1),
        name="peer_down",
    )(idx, act, x3, sel, spread, table)


def _rope_tables(pos):
    half = R_KEY_DIM // 2
    inv = ROPE_BASE ** (-jnp.arange(half, dtype=_F32) / half)
    ang = pos.astype(_F32)[:, None] * inv[None, :]
    cos, sin = jnp.cos(ang), jnp.sin(ang)
    return jnp.concatenate([cos, cos], axis=1), jnp.concatenate([-sin, sin], axis=1)


def _pick_tile(n, want):
    while n % want:
        want //= 2
    return want


def _layer(x, pos, weights, attend, s0, ret_len):
    (mix_g, w_in_b, hsum, qg, kg, r_gain, wa, wr, wo, fg, wq, keys_b, tbl_u, tbl_v) = weights
    b, t, d = x.shape
    n = b * t
    x2 = x.reshape(n, d)
    tm = _pick_tile(n, 512)
    cs, sn = _rope_tables(jnp.tile(pos, max(tm // t, 1)))
    qa, ka, va, qr, kr, vr, gr, ga, gt = _in_proj(x2, mix_g, w_in_b, hsum, qg, kg, cs, sn, tm)
    shp = lambda a: a.reshape(b, t, a.shape[1])
    oa = attend(shp(qa), shp(ka), shp(va))
    og, s_new = _retention(shp(qr), shp(kr), shp(vr), shp(gr), s0, r_gain, ret_len)
    x1, h2, qry = _merge(x2, oa.reshape(n, -1), og.reshape(n, -1), ga, gt, wa, wr, wo, fg, wq, tm)
    tr = _pick_tile(n, 256)
    idx, gate = _peer_route(qry, keys_b, tr)
    tp = _pick_tile(n, 128)
    hw = _pack_rows(h2).reshape(n, EXPERT_ROWS, LANES)
    act = _peer_up(idx, jnp.concatenate([hw, hw], axis=1), gate, tbl_u, tp)
    y = _peer_down(idx, act, x1.reshape(n, SUBLANES, LANES), tbl_v, tp)
    return y.reshape(b, t, d), shp(ka), shp(va), s_new


def kernel(x_prompt, x_sample, cache_attn_k, cache_attn_v, state_retention, mix_norm_g, w_in, attn_q_norm_g, attn_k_norm_g, attn_rel_bias, ret_out_norm_g, w_branch_attn, w_branch_ret, w_out, ffn_norm_g, peer_w_query, peer_sub_keys, peer_expert_u, peer_expert_v):
    depth = w_in.shape[0]
    assert depth == 1, "single-layer step"
    bp, s, d = x_prompt.shape
    bs, t, _ = x_sample.shape
    l = 0
    hsum = jnp.asarray(np.kron(np.eye(A_HEADS), np.ones((A_HEAD_DIM, A_HEAD_DIM))), _BF16)
    weights = (
        mix_norm_g[l][None, :], w_in[l].astype(_BF16), hsum,
        jnp.tile(attn_q_norm_g[l], A_HEADS)[None, :], jnp.tile(attn_k_norm_g[l], A_HEADS)[None, :],
        ret_out_norm_g[l].reshape(1, R_WIDTH),
        w_branch_attn[l].astype(_BF16), w_branch_ret[l].astype(_BF16), w_out[l].astype(_BF16),
        ffn_norm_g[l][None, :], peer_w_query[l].astype(_BF16),
        peer_sub_keys[l].reshape(2 * P_HEADS, P_KEYS, P_HALF).astype(_BF16),
        _pack_rows(peer_expert_u[l]), _pack_rows(peer_expert_v[l]),
    )
    rel_bias = attn_rel_bias[l]

    tq = 256
    n_past = BAND_PAST // tq
    rel_p = _rel_bias_row(rel_bias, tq, BAND_PAST + tq)

    def attend_prompt(q, k, v):
        def slots(a):
            return [(a, tq, functools.partial(
                lambda bi, i, back: (bi, jnp.maximum(i - back, 0), 0), back=n_past - sl))
                for sl in range(n_past + 1)]
        return _band_attn(q, slots(k), slots(v), rel_p, tq, BAND_PAST + tq, mask_leading=True)

    rows = cache_attn_k.shape[2]
    ck = cache_attn_k[l].reshape(bs, rows, A_WIDTH)
    cv = cache_attn_v[l].reshape(bs, rows, A_WIDTH)
    rel_s = _rel_bias_row(rel_bias, t, rows + t)

    def attend_sample(q, k, v):
        zero = lambda bi, i: (bi, 0, 0)
        return _band_attn(q, [(ck, rows, zero), (k, t, zero)], [(cv, rows, zero), (v, t, zero)],
                          rel_s, t, rows + t, mask_leading=False)

    s0_p = jnp.zeros((bp, R_HEADS, R_KEY_DIM, R_KEY_DIM), _F32)
    y_p, k_p, v_p, s_p = _layer(x_prompt, jnp.arange(s), weights, attend_prompt, s0_p, 256)
    y_s, k_s, v_s, s_s = _layer(x_sample, PAST_LEN + jnp.arange(t), weights, attend_sample,
                                state_retention[l], t)
    keep = min(BAND_PAST, s)
    heads = lambda a: a.reshape(a.shape[0], a.shape[1], A_HEADS, A_HEAD_DIM)[None]
    return (y_p, y_s, heads(k_p[:, s - keep:]), heads(v_p[:, s - keep:]), s_p[None],
            heads(k_s), heads(v_s), s_s[None])
```
